```python
import math
import jax, jax.numpy as jnp
from jax import lax
import numpy as np


D_MODEL = 1024
BATCH = 4
SEQ = 4096
DEPTH = 1

GRID_W = 64
NA_HEADS = 8
NA_DH = 64
NA_ROWS = 8
NA_COLS = 16
DF_HEADS = 4
DF_DH = 64
Q_BLOCK = 128
NA_WIDTH = NA_HEADS * NA_DH
DF_WIDTH = DF_HEADS * 2 * DF_DH
MIX_WIDTH = NA_WIDTH + DF_WIDTH
IN_COLS = 3 * NA_WIDTH + 3 * DF_WIDTH
D_FF = 2816
DEEPNORM_ALPHA = (2.0 * DEPTH) ** 0.25
DEEPNORM_BETA = (8.0 * DEPTH) ** -0.25
LN_EPS = 1e-5

kernel_name = "hybrid_na_diffattn_macaron_deepnorm"


def layer_norm(x, g, b):
    xf = x.astype(jnp.float32)
    mu = jnp.mean(xf, axis=-1, keepdims=True)
    xc = xf - mu
    var = jnp.mean(xc * xc, axis=-1, keepdims=True)
    return (xc * lax.rsqrt(var + LN_EPS) * g + b).astype(x.dtype)


def rms_norm(x, g):
    xf = x.astype(jnp.float32)
    y = xf * lax.rsqrt(jnp.mean(xf * xf, axis=-1, keepdims=True) + LN_EPS)
    return (y * g).astype(x.dtype)


def swiglu(x, w_gate, w_up, w_down):
    return (jax.nn.silu(x @ w_gate) * (x @ w_up)) @ w_down


def neighbourhood_attention(q, k, v, rpb):
    B, T, H, dh = q.shape
    rows = T // GRID_W
    wr = min(NA_ROWS, rows)
    scale = dh ** -0.5
    qg = q.reshape(B, rows, GRID_W, H, dh).transpose(1, 0, 3, 2, 4)
    kg = k.reshape(B, rows, GRID_W, H, dh).transpose(0, 3, 1, 2, 4)
    vg = v.reshape(B, rows, GRID_W, H, dh).transpose(0, 3, 1, 2, 4)
    cols = jnp.arange(GRID_W)
    c0 = jnp.clip(cols - NA_COLS // 2, 0, GRID_W - NA_COLS)
    col_idx = c0[:, None] + jnp.arange(NA_COLS)
    dc = col_idx - cols[:, None] + (NA_COLS - 1)

    def row_block(args):
        r, qr = args
        r0 = jnp.clip(r - wr // 2, 0, rows - wr)
        kb = lax.dynamic_slice_in_dim(kg, r0, wr, axis=2)[:, :, :, col_idx]
        vb = lax.dynamic_slice_in_dim(vg, r0, wr, axis=2)[:, :, :, col_idx]
        dr = r0 + jnp.arange(wr) - r + (NA_ROWS - 1)
        bias = rpb[:, dr[None, :, None], dc[:, None, :]]
        s = jnp.einsum('bhcd,bhwcjd->bhcwj', qr, kb).astype(jnp.float32) * scale + bias
        p = jax.nn.softmax(s.reshape(B, H, GRID_W, wr * NA_COLS), axis=-1)
        p = p.reshape(B, H, GRID_W, wr, NA_COLS).astype(v.dtype)
        return jnp.einsum('bhcwj,bhwcjd->bhcd', p, vb)

    out = lax.map(row_block, (jnp.arange(rows), qg))
    return out.transpose(1, 0, 3, 2, 4).reshape(B, T, H * dh)


def differential_attention(q, k, v, lam, subln_g, lam_init):
    B, T, H, _, dh = q.shape
    nb = T // Q_BLOCK
    scale = dh ** -0.5
    slopes = jnp.exp2(-8.0 * jnp.arange(1, H + 1, dtype=jnp.float32) / H)
    qb = q.reshape(B, nb, Q_BLOCK, H, 2, dh).transpose(1, 0, 2, 3, 4, 5)
    kpos = jnp.arange(T)

    def block(args):
        i, qi = args
        qpos = i * Q_BLOCK + jnp.arange(Q_BLOCK)
        dist = jnp.abs(qpos[:, None] - kpos[None, :]).astype(jnp.float32)
        alibi = -slopes[:, None, None] * dist
        s = jnp.einsum('bqhmd,bkhmd->bhmqk', qi, k).astype(jnp.float32) * scale
        p = jax.nn.softmax(s + alibi[None, :, None], axis=-1)
        a = (p[:, :, 0] - lam * p[:, :, 1]).astype(v.dtype)
        return jnp.einsum('bhqk,bkhe->bqhe', a, v)

    o = lax.map(block, (jnp.arange(nb), qb))
    o = o.transpose(1, 0, 2, 3, 4).reshape(B, T, H, 2 * dh)
    o = rms_norm(o, subln_g) * (1.0 - lam_init)
    return o.reshape(B, T, H * 2 * dh)


def hybrid_mixer(h, w_in, w_out, na_rpb, lq1, lk1, lq2, lk2, subln_g, lam_init):
    B, T, _ = h.shape
    p = h @ w_in
    splits = np.cumsum([NA_WIDTH, NA_WIDTH, NA_WIDTH, DF_WIDTH, DF_WIDTH])
    na_q, na_k, na_v, df_q, df_k, df_v = jnp.split(p, splits, axis=-1)
    na_out = neighbourhood_attention(
        na_q.reshape(B, T, NA_HEADS, NA_DH),
        na_k.reshape(B, T, NA_HEADS, NA_DH),
        na_v.reshape(B, T, NA_HEADS, NA_DH), na_rpb)
    lam = (jnp.exp(jnp.sum(lq1.astype(jnp.float32) * lk1.astype(jnp.float32)))
           - jnp.exp(jnp.sum(lq2.astype(jnp.float32) * lk2.astype(jnp.float32)))
           + lam_init)
    df_out = differential_attention(
        df_q.reshape(B, T, DF_HEADS, 2, DF_DH),
        df_k.reshape(B, T, DF_HEADS, 2, DF_DH),
        df_v.reshape(B, T, DF_HEADS, 2 * DF_DH), lam, subln_g, lam_init)
    return jnp.concatenate([na_out, df_out], axis=-1) @ w_out


def setup_inputs(seed: int = 0) -> dict:
    key = jax.random.key(seed)
    ks = jax.random.split(key, 24)
    D, L = D_MODEL, DEPTH

    def nrm(k, shape, scale):
        return jax.random.normal(k, shape, jnp.float32) * scale

    beta = DEEPNORM_BETA
    col_scale = jnp.concatenate([
        jnp.ones((2 * NA_WIDTH,), jnp.float32), jnp.full((NA_WIDTH,), beta, jnp.float32),
        jnp.ones((2 * DF_WIDTH,), jnp.float32), jnp.full((DF_WIDTH,), beta, jnp.float32)])
    return {
        "x": nrm(ks[0], (BATCH, SEQ, D), 1.0),
        "ln1_g": 1.0 + nrm(ks[1], (L, D), 0.02),
        "ln1_b": nrm(ks[2], (L, D), 0.02),
        "ffn1_w_gate": nrm(ks[3], (L, D, D_FF), beta * D ** -0.5),
        "ffn1_w_up": nrm(ks[4], (L, D, D_FF), beta * D ** -0.5),
        "ffn1_w_down": nrm(ks[5], (L, D_FF, D), beta * D_FF ** -0.5),
        "w_in": nrm(ks[6], (L, D, IN_COLS), D ** -0.5) * col_scale,
        "na_rpb": nrm(ks[7], (L, NA_HEADS, 2 * NA_ROWS - 1, 2 * NA_COLS - 1), 0.1),
        "diff_lambda_q1": nrm(ks[8], (L, DF_DH), 0.1),
        "diff_lambda_k1": nrm(ks[9], (L, DF_DH), 0.1),
        "diff_lambda_q2": nrm(ks[10], (L, DF_DH), 0.1),
        "diff_lambda_k2": nrm(ks[11], (L, DF_DH), 0.1),
        "diff_subln_g": 1.0 + nrm(ks[12], (L, 2 * DF_DH), 0.02),
        "w_out": nrm(ks[13], (L, MIX_WIDTH, D), beta * MIX_WIDTH ** -0.5),
        "ln2_g": 1.0 + nrm(ks[14], (L, D), 0.02),
        "ln2_b": nrm(ks[15], (L, D), 0.02),
        "ffn2_w_gate": nrm(ks[16], (L, D, D_FF), beta * D ** -0.5),
        "ffn2_w_up": nrm(ks[17], (L, D, D_FF), beta * D ** -0.5),
        "ffn2_w_down": nrm(ks[18], (L, D_FF, D), beta * D_FF ** -0.5),
        "ln3_g": 1.0 + nrm(ks[19], (L, D), 0.02),
        "ln3_b": nrm(ks[20], (L, D), 0.02),
    }


def reference(x, ln1_g, ln1_b, ffn1_w_gate, ffn1_w_up, ffn1_w_down, w_in, na_rpb,
              diff_lambda_q1, diff_lambda_k1, diff_lambda_q2, diff_lambda_k2, diff_subln_g,
              w_out, ln2_g, ln2_b, ffn2_w_gate, ffn2_w_up, ffn2_w_down, ln3_g, ln3_b):
    a = DEEPNORM_ALPHA
    for l in range(DEPTH):
        lam_init = 0.8 - 0.6 * math.exp(-0.3 * l)
        x = layer_norm(a * x + 0.5 * swiglu(x, ffn1_w_gate[l], ffn1_w_up[l], ffn1_w_down[l]),
                       ln1_g[l], ln1_b[l])
        mix = hybrid_mixer(x, w_in[l], w_out[l], na_rpb[l],
                           diff_lambda_q1[l], diff_lambda_k1[l],
                           diff_lambda_q2[l], diff_lambda_k2[l],
                           diff_subln_g[l], lam_init)
        x = layer_norm(a * x + mix, ln2_g[l], ln2_b[l])
        x = layer_norm(a * x + 0.5 * swiglu(x, ffn2_w_gate[l], ffn2_w_up[l], ffn2_w_down[l]),
                       ln3_g[l], ln3_b[l])
    return x
```

```python
import functools
import math

import jax
import jax.numpy as jnp
from jax import lax
from jax.experimental import pallas as pl
from jax.experimental.pallas import tpu as pltpu

F32 = jnp.float32
BF16 = jnp.bfloat16

D_MODEL = 1024
D_FF = 2816
DEPTH = 1
GRID_W = 64
NA_HEADS, NA_DH = 8, 64
NA_ROWS, NA_COLS = 8, 16
DF_HEADS, DF_DH = 4, 64
NA_WIDTH = NA_HEADS * NA_DH
DF_WIDTH = DF_HEADS * 2 * DF_DH
IN_COLS = 3 * NA_WIDTH + 3 * DF_WIDTH
ALPHA = (2.0 * DEPTH) ** 0.25
LN_EPS = 1e-5
LAM_INIT = 0.8 - 0.6 * math.exp(-0.3 * 0)
NEG_BIG = -1e30

LANES = 128
MXU_DIM = 256
FF_CHUNK = MXU_DIM
N_FF_CHUNKS = D_FF // FF_CHUNK
FFN_ROWS = 512
NA_ROWS_PER_STEP = 8
DF_TQ = 256
DF_TK = 512
VMEM_LIMIT = 56 * 1024 * 1024


def _layer_norm(y, g, b):
    mu = jnp.mean(y, axis=-1, keepdims=True)
    yc = y - mu
    var = jnp.mean(yc * yc, axis=-1, keepdims=True)
    return yc * lax.rsqrt(var + LN_EPS) * g + b


def _swiglu_into(xb_ref, wg_ref, wu_ref, wd_ref, acc_ref):
    acc_ref[...] = jnp.zeros_like(acc_ref)

    def chunk(c, carry):
        xb = xb_ref[...]
        g = jnp.dot(xb, wg_ref[c], preferred_element_type=F32)
        u = jnp.dot(xb, wu_ref[c], preferred_element_type=F32)
        h = (g * jax.nn.sigmoid(g) * u).astype(BF16)
        acc_ref[...] += jnp.dot(h, wd_ref[c], preferred_element_type=F32)
        return carry

    lax.fori_loop(0, N_FF_CHUNKS, chunk, 0)


def _ffn1_kernel(x_ref, wg_ref, wu_ref, wd_ref, g_ref, b_ref, win_ref,
                 x1_ref, p_ref, xb_ref, acc_ref):
    xb_ref[...] = x_ref[...].astype(BF16)
    _swiglu_into(xb_ref, wg_ref, wu_ref, wd_ref, acc_ref)
    x1 = _layer_norm(ALPHA * x_ref[...] + 0.5 * acc_ref[...], g_ref[...], b_ref[...])
    x1_ref[...] = x1
    xb_ref[...] = x1.astype(BF16)
    step = 2 * MXU_DIM
    for j in range(IN_COLS // step):
        cols = slice(j * step, (j + 1) * step)
        p_ref[:, cols] = jnp.dot(xb_ref[...], win_ref[:, cols],
                                 preferred_element_type=F32).astype(BF16)


def _ffn2_kernel(x1_ref, na_ref, df_ref, wout_ref, g2_ref, b2_ref,
                 wg_ref, wu_ref, wd_ref, g3_ref, b3_ref,
                 o_ref, x2_ref, xb_ref, acc_ref):
    mix = jnp.dot(na_ref[...], wout_ref[:NA_WIDTH, :], preferred_element_type=F32)
    mix += jnp.dot(df_ref[...], wout_ref[NA_WIDTH:, :], preferred_element_type=F32)
    x2 = _layer_norm(ALPHA * x1_ref[...] + mix, g2_ref[...], b2_ref[...])
    x2_ref[...] = x2
    xb_ref[...] = x2.astype(BF16)
    _swiglu_into(xb_ref, wg_ref, wu_ref, wd_ref, acc_ref)
    o_ref[...] = _layer_norm(ALPHA * x2_ref[...] + 0.5 * acc_ref[...],
                             g3_ref[...], b3_ref[...])


def _na_kernel(q_ref, k_ref, v_ref, tbl_ref, o_ref):
    rows_total = k_ref.shape[0] // GRID_W
    win_keys = NA_ROWS * GRID_W
    lane = lax.broadcasted_iota(jnp.int32, (GRID_W, LANES), 1)
    lo = lane < NA_DH
    scale = NA_DH ** -0.5

    def row(rr, carry):
        r = pl.program_id(1) * NA_ROWS_PER_STEP + rr
        r0 = jnp.clip(r - NA_ROWS // 2, 0, rows_total - NA_ROWS)
        e = r - r0
        qrows = pl.ds(pl.multiple_of(rr * GRID_W, GRID_W), GRID_W)
        krows = pl.ds(pl.multiple_of(r0 * GRID_W, GRID_W), win_keys)
        for hp in range(NA_HEADS // 2):
            cols = slice(hp * LANES, (hp + 1) * LANES)
            q = q_ref[qrows, cols] * jnp.asarray(scale, BF16)
            zero = jnp.zeros_like(q)
            qz = jnp.concatenate([jnp.where(lo, q, zero), jnp.where(lo, zero, q)], axis=0)
            kw = k_ref[krows, cols]
            s = lax.dot_general(qz, kw, (((1,), (1,)), ((), ())),
                                preferred_element_type=F32)
            s = s + tbl_ref[hp, e]
            m = jnp.max(s, axis=-1, keepdims=True)
            p = jnp.exp(s - m)
            l = jnp.sum(p, axis=-1, keepdims=True)
            o = jnp.dot(p.astype(BF16), v_ref[krows, cols], preferred_element_type=F32)
            o = o / l
            o_ref[qrows, cols] = jnp.where(lo, o[:GRID_W], o[GRID_W:]).astype(o_ref.dtype)
        return carry

    lax.fori_loop(0, NA_ROWS_PER_STEP, row, 0)


def _diff_kernel(slope_ref, q_ref, k_ref, v_ref, lq1_ref, lk1_ref, lq2_ref, lk2_ref, g_ref,
                 o_ref, qz_ref, m_ref, l_ref, acc_ref):
    tq = q_ref.shape[0]
    t_keys = k_ref.shape[0]
    h = pl.program_id(1)
    i = pl.program_id(2)
    lane = lax.broadcasted_iota(jnp.int32, (tq, LANES), 1)
    q = q_ref[...] * jnp.asarray(DF_DH ** -0.5, BF16)
    zero = jnp.zeros_like(q)
    qz_ref[:tq, :] = jnp.where(lane < DF_DH, q, zero)
    qz_ref[tq:, :] = jnp.where(lane < DF_DH, zero, q)
    m_ref[...] = jnp.full(m_ref.shape, NEG_BIG, F32)
    l_ref[...] = jnp.zeros_like(l_ref)
    acc_ref[...] = jnp.zeros_like(acc_ref)
    slope = slope_ref[h]
    rel = (lax.broadcasted_iota(jnp.int32, (tq, DF_TK), 0)
           - lax.broadcasted_iota(jnp.int32, (tq, DF_TK), 1)).astype(F32)

    def kv_step(j, carry):
        krows = pl.ds(pl.multiple_of(j * DF_TK, DF_TK), DF_TK)
        s = lax.dot_general(qz_ref[...], k_ref[krows, :], (((1,), (1,)), ((), ())),
                            preferred_element_type=F32)
        off = (i * tq - j * DF_TK).astype(F32)
        alibi = -slope * jnp.abs(rel + off)
        vc = v_ref[krows, :]
        for half in range(2):
            rows = slice(half * tq, (half + 1) * tq)
            sh = s[rows] + alibi
            m_prev = m_ref[rows, :]
            m_new = jnp.maximum(m_prev, jnp.max(sh, axis=-1, keepdims=True))
            p = jnp.exp(sh - m_new)
            corr = jnp.exp(m_prev - m_new)
            l_ref[rows, :] = corr * l_ref[rows, :] + jnp.sum(p, axis=-1, keepdims=True)
            acc_ref[rows, :] = corr * acc_ref[rows, :] + jnp.dot(
                p.astype(BF16), vc, preferred_element_type=F32)
            m_ref[rows, :] = m_new
        return carry

    lax.fori_loop(0, t_keys // DF_TK, kv_step, 0)

    lam = (jnp.exp(jnp.sum(lq1_ref[...] * lk1_ref[...], axis=-1, keepdims=True))
           - jnp.exp(jnp.sum(lq2_ref[...] * lk2_ref[...], axis=-1, keepdims=True))
           + LAM_INIT)
    o = acc_ref[:tq, :] / l_ref[:tq, :] - lam * (acc_ref[tq:, :] / l_ref[tq:, :])
    y = o * lax.rsqrt(jnp.mean(o * o, axis=-1, keepdims=True) + LN_EPS)
    o_ref[...] = (y * g_ref[...] * (1.0 - LAM_INIT)).astype(o_ref.dtype)


def _na_bias_table(rpb):
    c = jnp.arange(GRID_W)
    c0 = jnp.clip(c - NA_COLS // 2, 0, GRID_W - NA_COLS)
    cc = jnp.arange(GRID_W)
    valid = (cc[None, :] >= c0[:, None]) & (cc[None, :] < c0[:, None] + NA_COLS)
    dc = jnp.clip(cc[None, :] - c[:, None] + (NA_COLS - 1), 0, 2 * NA_COLS - 2)
    e = jnp.arange(NA_ROWS)
    j = jnp.arange(NA_ROWS)
    dr = j[None, :] - e[:, None] + (NA_ROWS - 1)
    t = rpb[:, dr[:, :, None, None], dc[None, None, :, :]]
    t = jnp.where(valid[None, None, None], t, NEG_BIG)
    t = t.transpose(0, 1, 3, 2, 4).reshape(NA_HEADS, NA_ROWS, GRID_W, NA_ROWS * GRID_W)
    t = t.reshape(NA_HEADS // 2, 2, NA_ROWS, GRID_W, NA_ROWS * GRID_W)
    return t.transpose(0, 2, 1, 3, 4).reshape(NA_HEADS // 2, NA_ROWS, 2 * GRID_W,
                                              NA_ROWS * GRID_W)


def _chunk_cols(w):
    return w.astype(BF16).reshape(w.shape[0], N_FF_CHUNKS, FF_CHUNK).transpose(1, 0, 2)


def _chunk_rows(w):
    return w.astype(BF16).reshape(N_FF_CHUNKS, FF_CHUNK, w.shape[1])


def _resident(shape):
    return pl.BlockSpec(shape, lambda *_: (0,) * len(shape), pipeline_mode=pl.Buffered(1))


def _params(n_axes):
    return pltpu.CompilerParams(dimension_semantics=("arbitrary",) * n_axes,
                                vmem_limit_bytes=VMEM_LIMIT)


def _ffn_weight_specs():
    return [_resident((N_FF_CHUNKS, D_MODEL, FF_CHUNK)),
            _resident((N_FF_CHUNKS, D_MODEL, FF_CHUNK)),
            _resident((N_FF_CHUNKS, FF_CHUNK, D_MODEL))]


def _ffn1_call(x2d, wg, wu, wd, g, b, w_in):
    n = x2d.shape[0]
    row_spec = pl.BlockSpec((FFN_ROWS, D_MODEL), lambda i: (i, 0))
    vec_spec = _resident((1, D_MODEL))
    return pl.pallas_call(
        _ffn1_kernel,
        grid=(n // FFN_ROWS,),
        in_specs=[row_spec, *_ffn_weight_specs(), vec_spec, vec_spec,
                  _resident((D_MODEL, IN_COLS))],
        out_specs=[row_spec, pl.BlockSpec((FFN_ROWS, IN_COLS), lambda i: (i, 0))],
        out_shape=[jax.ShapeDtypeStruct((n, D_MODEL), F32),
                   jax.ShapeDtypeStruct((n, IN_COLS), BF16)],
        scratch_shapes=[pltpu.VMEM((FFN_ROWS, D_MODEL), BF16),
                        pltpu.VMEM((FFN_ROWS, D_MODEL), F32)],
        compiler_params=_params(1),
        name="ffn1_ln1_qkv",
    )(x2d, wg, wu, wd, g, b, w_in)


def _ffn2_call(x1, na_out, df_out, w_out, g2, b2, wg, wu, wd, g3, b3):
    n = x1.shape[0]
    row_spec = pl.BlockSpec((FFN_ROWS, D_MODEL), lambda i: (i, 0))
    half_spec = pl.BlockSpec((FFN_ROWS, NA_WIDTH), lambda i: (i, 0))
    vec_spec = _resident((1, D_MODEL))
    return pl.pallas_call(
        _ffn2_kernel,
        grid=(n // FFN_ROWS,),
        in_specs=[row_spec, half_spec, half_spec, _resident((D_MODEL, D_MODEL)),
                  vec_spec, vec_spec, *_ffn_weight_specs(), vec_spec, vec_spec],
        out_specs=row_spec,
        out_shape=jax.ShapeDtypeStruct((n, D_MODEL), F32),
        scratch_shapes=[pltpu.VMEM((FFN_ROWS, D_MODEL), F32),
                        pltpu.VMEM((FFN_ROWS, D_MODEL), BF16),
                        pltpu.VMEM((FFN_ROWS, D_MODEL), F32)],
        compiler_params=_params(1),
        name="outproj_ln2_ffn2_ln3",
    )(x1, na_out, df_out, w_out, g2, b2, wg, wu, wd, g3, b3)


def _na_call(p, tbl, batch, seq):
    step_rows = NA_ROWS_PER_STEP * GRID_W
    steps = seq // step_rows
    q_spec = pl.BlockSpec((step_rows, NA_WIDTH), lambda b, i: (b * steps + i, 0))
    return pl.pallas_call(
        _na_kernel,
        grid=(batch, steps),
        in_specs=[q_spec,
                  pl.BlockSpec((seq, NA_WIDTH), lambda b, i: (b, 1)),
                  pl.BlockSpec((seq, NA_WIDTH), lambda b, i: (b, 2)),
                  _resident(tbl.shape)],
        out_specs=q_spec,
        out_shape=jax.ShapeDtypeStruct((batch * seq, NA_WIDTH), BF16),
        compiler_params=_params(2),
        name="neighbourhood_attention",
    )(p, p, p, tbl)


def _diff_call(p, slopes, lq1, lk1, lq2, lk2, subln_g, batch, seq):
    steps = seq // DF_TQ
    q_col0 = 3 * NA_WIDTH // LANES
    k_col0 = q_col0 + DF_WIDTH // LANES
    v_col0 = k_col0 + DF_WIDTH // LANES
    lam_spec = _resident((1, DF_DH))
    return pl.pallas_call(
        _diff_kernel,
        grid=(batch, DF_HEADS, steps),
        in_specs=[pl.BlockSpec(memory_space=pltpu.SMEM),
                  pl.BlockSpec((DF_TQ, LANES), lambda b, h, i: (b * steps + i, q_col0 + h)),
                  pl.BlockSpec((seq, LANES), lambda b, h, i: (b, k_col0 + h)),
                  pl.BlockSpec((seq, LANES), lambda b, h, i: (b, v_col0 + h)),
                  lam_spec, lam_spec, lam_spec, lam_spec,
                  _resident((1, 2 * DF_DH))],
        out_specs=pl.BlockSpec((DF_TQ, LANES), lambda b, h, i: (b * steps + i, h)),
        out_shape=jax.ShapeDtypeStruct((batch * seq, DF_WIDTH), BF16),
        scratch_shapes=[pltpu.VMEM((2 * DF_TQ, LANES), BF16),
                        pltpu.VMEM((2 * DF_TQ, 1), F32),
                        pltpu.VMEM((2 * DF_TQ, 1), F32),
                        pltpu.VMEM((2 * DF_TQ, LANES), F32)],
        compiler_params=_params(3),
        name="differential_attention",
    )(slopes, p, p, p, lq1, lk1, lq2, lk2, subln_g)


def kernel(x, ln1_g, ln1_b, ffn1_w_gate, ffn1_w_up, ffn1_w_down, w_in, na_rpb,
           diff_lambda_q1, diff_lambda_k1, diff_lambda_q2, diff_lambda_k2, diff_subln_g,
           w_out, ln2_g, ln2_b, ffn2_w_gate, ffn2_w_up, ffn2_w_down, ln3_g, ln3_b):
    batch, seq, d = x.shape
    assert d == D_MODEL and seq % (NA_ROWS_PER_STEP * GRID_W) == 0 and seq % DF_TK == 0
    assert ln1_g.shape[0] == DEPTH == 1
    x2d = x.reshape(batch * seq, d)
    x1, p = _ffn1_call(x2d, _chunk_cols(ffn1_w_gate[0]), _chunk_cols(ffn1_w_up[0]),
                       _chunk_rows(ffn1_w_down[0]), ln1_g, ln1_b, w_in[0].astype(BF16))
    na_out = _na_call(p, _na_bias_table(na_rpb[0]), batch, seq)
    slopes = jnp.exp2(-8.0 * jnp.arange(1, DF_HEADS + 1, dtype=F32) / DF_HEADS)
    df_out = _diff_call(p, slopes, diff_lambda_q1, diff_lambda_k1, diff_lambda_q2,
                        diff_lambda_k2, diff_subln_g, batch, seq)
    out = _ffn2_call(x1, na_out, df_out, w_out[0].astype(BF16), ln2_g, ln2_b,
                     _chunk_cols(ffn2_w_gate[0]), _chunk_cols(ffn2_w_up[0]),
                     _chunk_rows(ffn2_w_down[0]), ln3_g, ln3_b)
    return out.reshape(batch, seq, d)
```

```python
import functools
import math

import jax
import jax.numpy as jnp
from jax import lax
from jax.experimental import pallas as pl
from jax.experimental.pallas import tpu as pltpu

F32 = jnp.float32
BF16 = jnp.bfloat16

D_MODEL = 1024
D_FF = 2816
DEPTH = 1
GRID_W = 64
NA_HEADS, NA_DH = 8, 64
NA_ROWS, NA_COLS = 8, 16
DF_HEADS, DF_DH = 4, 64
NA_WIDTH = NA_HEADS * NA_DH
DF_WIDTH = DF_HEADS * 2 * DF_DH
IN_COLS = 3 * NA_WIDTH + 3 * DF_WIDTH
ALPHA = (2.0 * DEPTH) ** 0.25
LN_EPS = 1e-5
LAM_INIT = 0.8 - 0.6 * math.exp(-0.3 * 0)
NEG_BIG = -1e30
LOG2E = math.log2(math.e)

LANES = 128
MXU_DIM = 256
FF_CHUNK = MXU_DIM
N_FF_CHUNKS = D_FF // FF_CHUNK
FFN_ROWS = 512
NA_ROWS_PER_STEP = 8
DF_TQ = 256
DF_TK = 512
VMEM_LIMIT = 56 * 1024 * 1024


def _layer_norm(y, g, b):
    mu = jnp.mean(y, axis=-1, keepdims=True)
    yc = y - mu
    var = jnp.mean(yc * yc, axis=-1, keepdims=True)
    return yc * lax.rsqrt(var + LN_EPS) * g + b


def _swiglu_into(xb_ref, wg_ref, wu_ref, wd_ref, acc_ref):
    acc_ref[...] = jnp.zeros_like(acc_ref)

    def chunk(c, carry):
        xb = xb_ref[...]
        g = jnp.dot(xb, wg_ref[c], preferred_element_type=F32)
        u = jnp.dot(xb, wu_ref[c], preferred_element_type=F32)
        h = (g * jax.nn.sigmoid(g) * u).astype(BF16)
        acc_ref[...] += jnp.dot(h, wd_ref[c], preferred_element_type=F32)
        return carry

    lax.fori_loop(0, N_FF_CHUNKS, chunk, 0)


def _ffn1_kernel(x_ref, wg_ref, wu_ref, wd_ref, g_ref, b_ref, win_ref,
                 x1_ref, p_ref, xb_ref, acc_ref):
    xb_ref[...] = x_ref[...].astype(BF16)
    _swiglu_into(xb_ref, wg_ref, wu_ref, wd_ref, acc_ref)
    x1 = _layer_norm(ALPHA * x_ref[...] + 0.5 * acc_ref[...], g_ref[...], b_ref[...])
    x1_ref[...] = x1
    xb_ref[...] = x1.astype(BF16)
    group_scale = (NA_DH ** -0.5 * LOG2E, None, None, DF_DH ** -0.5 * LOG2E, None, None)
    for j, gs in enumerate(group_scale):
        cols = slice(j * NA_WIDTH, (j + 1) * NA_WIDTH)
        pj = jnp.dot(xb_ref[...], win_ref[:, cols], preferred_element_type=F32)
        if gs is not None:
            pj = pj * gs
        p_ref[:, cols] = pj.astype(BF16)


def _ffn2_kernel(x1_ref, na_ref, df_ref, wout_ref, g2_ref, b2_ref,
                 wg_ref, wu_ref, wd_ref, g3_ref, b3_ref,
                 o_ref, x2_ref, xb_ref, acc_ref):
    mix = jnp.dot(na_ref[...], wout_ref[:NA_WIDTH, :], preferred_element_type=F32)
    mix += jnp.dot(df_ref[...], wout_ref[NA_WIDTH:, :], preferred_element_type=F32)
    x2 = _layer_norm(ALPHA * x1_ref[...] + mix, g2_ref[...], b2_ref[...])
    x2_ref[...] = x2
    xb_ref[...] = x2.astype(BF16)
    _swiglu_into(xb_ref, wg_ref, wu_ref, wd_ref, acc_ref)
    o_ref[...] = _layer_norm(ALPHA * x2_ref[...] + 0.5 * acc_ref[...],
                             g3_ref[...], b3_ref[...])


def _na_kernel(q_ref, k_ref, v_ref, tbl_ref, o_ref):
    rows_total = k_ref.shape[0] // GRID_W
    win_keys = NA_ROWS * GRID_W
    lane = lax.broadcasted_iota(jnp.int32, (GRID_W, LANES), 1)
    lo = lane < NA_DH

    def row(rr, carry):
        r = pl.program_id(1) * NA_ROWS_PER_STEP + rr
        r0 = jnp.clip(r - NA_ROWS // 2, 0, rows_total - NA_ROWS)
        e = r - r0
        qrows = pl.ds(pl.multiple_of(rr * GRID_W, GRID_W), GRID_W)
        krows = pl.ds(pl.multiple_of(r0 * GRID_W, GRID_W), win_keys)
        for hp in range(NA_HEADS // 2):
            cols = slice(hp * LANES, (hp + 1) * LANES)
            q = q_ref[qrows, cols]
            zero = jnp.zeros_like(q)
            qz = jnp.concatenate([jnp.where(lo, q, zero), jnp.where(lo, zero, q)], axis=0)
            kw = k_ref[krows, cols]
            s = lax.dot_general(qz, kw, (((1,), (1,)), ((), ())),
                                preferred_element_type=F32)
            s = s + tbl_ref[hp, e]
            m = jnp.max(s, axis=-1, keepdims=True)
            p = jnp.exp2(s - m)
            l = jnp.sum(p, axis=-1, keepdims=True)
            o = jnp.dot(p.astype(BF16), v_ref[krows, cols], preferred_element_type=F32)
            o = o / l
            o_ref[qrows, cols] = jnp.where(lo, o[:GRID_W], o[GRID_W:]).astype(o_ref.dtype)
        return carry

    lax.fori_loop(0, NA_ROWS_PER_STEP, row, 0)


def _diff_kernel(slope_ref, q_ref, k_ref, v_ref, lq1_ref, lk1_ref, lq2_ref, lk2_ref, g_ref,
                 o_ref, qz_ref, alibi_ref):
    tq = q_ref.shape[0]
    n_chunks = k_ref.shape[0] // DF_TK
    h = pl.program_id(1)
    i = pl.program_id(2)
    lane = lax.broadcasted_iota(jnp.int32, (tq, LANES), 1)
    q = q_ref[...]
    zero = jnp.zeros_like(q)
    qz_ref[:tq, :] = jnp.where(lane < DF_DH, q, zero)
    qz_ref[tq:, :] = jnp.where(lane < DF_DH, zero, q)

    a = slope_ref[h] * LOG2E
    rel = (lax.broadcasted_iota(jnp.int32, (tq, DF_TK), 0)
           - lax.broadcasted_iota(jnp.int32, (tq, DF_TK), 1)).astype(F32)
    j_diag = (i * tq) // DF_TK
    alibi_ref[0] = -a * rel
    alibi_ref[1] = a * rel
    alibi_ref[2] = -a * jnp.abs(rel + (i * tq - j_diag * DF_TK).astype(F32))

    m = [jnp.full((tq, 1), NEG_BIG, F32)] * 2
    l = [jnp.zeros((tq, 1), F32)] * 2
    acc = [jnp.zeros((tq, LANES), F32)] * 2
    for j in range(n_chunks):
        kind = jnp.where(j < j_diag, 0, jnp.where(j > j_diag, 1, 2))
        off = (i * tq - j * DF_TK).astype(F32)
        c = jnp.where(j == j_diag, 0.0, -a * jnp.abs(off))
        krows = slice(j * DF_TK, (j + 1) * DF_TK)
        s = lax.dot_general(qz_ref[...], k_ref[krows, :], (((1,), (1,)), ((), ())),
                            preferred_element_type=F32)
        tile = alibi_ref[kind]
        vc = v_ref[krows, :]
        for half in range(2):
            t = s[half * tq:(half + 1) * tq] + tile
            m_new = jnp.maximum(m[half], jnp.max(t, axis=-1, keepdims=True) + c)
            p = jnp.exp2(t - (m_new - c))
            corr = jnp.exp2(m[half] - m_new)
            l[half] = corr * l[half] + jnp.sum(p, axis=-1, keepdims=True)
            acc[half] = corr * acc[half] + jnp.dot(p.astype(BF16), vc,
                                                   preferred_element_type=F32)
            m[half] = m_new

    lam = (jnp.exp(jnp.sum(lq1_ref[...] * lk1_ref[...], axis=-1, keepdims=True))
           - jnp.exp(jnp.sum(lq2_ref[...] * lk2_ref[...], axis=-1, keepdims=True))
           + LAM_INIT)
    o = acc[0] / l[0] - lam * (acc[1] / l[1])
    y = o * lax.rsqrt(jnp.mean(o * o, axis=-1, keepdims=True) + LN_EPS)
    o_ref[...] = (y * g_ref[...] * (1.0 - LAM_INIT)).astype(o_ref.dtype)


def _na_table_kernel(rpb_ref, o_ref):
    pair = pl.program_id(0)
    e = pl.program_id(1)
    n_dr, n_dc = 2 * NA_ROWS - 1, 2 * NA_COLS - 1
    c = lax.broadcasted_iota(jnp.int32, (GRID_W, LANES), 0)
    lane = lax.broadcasted_iota(jnp.int32, (GRID_W, LANES), 1)
    cc = lane & (GRID_W - 1)
    c0 = jnp.clip(c - NA_COLS // 2, 0, GRID_W - NA_COLS)
    in_window = (cc >= c0) & (cc < c0 + NA_COLS)
    dc = jnp.where(in_window, cc - c + (NA_COLS - 1), -1)
    upper = lane >= GRID_W
    for half in range(2):
        head = 2 * pair + half
        for jj in range(NA_ROWS // 2):
            base = (head * n_dr + (2 * jj - e + NA_ROWS - 1)) * n_dc
            acc = jnp.full((GRID_W, LANES), NEG_BIG, F32)
            for x in range(n_dc):
                val = jnp.where(upper, rpb_ref[base + n_dc + x], rpb_ref[base + x])
                acc = jnp.where(dc == x, val, acc)
            o_ref[0, 0, half * GRID_W:(half + 1) * GRID_W, jj * LANES:(jj + 1) * LANES] = (
                acc * LOG2E)


def _na_bias_table(rpb):
    tile = (2 * GRID_W, NA_ROWS * GRID_W)
    return pl.pallas_call(
        _na_table_kernel,
        grid=(NA_HEADS // 2, NA_ROWS),
        in_specs=[pl.BlockSpec(memory_space=pltpu.SMEM)],
        out_specs=pl.BlockSpec((1, 1, *tile), lambda a, e: (a, e, 0, 0)),
        out_shape=jax.ShapeDtypeStruct((NA_HEADS // 2, NA_ROWS, *tile), F32),
        compiler_params=_params(2),
        name="na_bias_table",
    )(rpb.reshape(-1))


def _chunk_cols(w):
    return w.astype(BF16).reshape(w.shape[0], N_FF_CHUNKS, FF_CHUNK).transpose(1, 0, 2)


def _chunk_rows(w):
    return w.astype(BF16).reshape(N_FF_CHUNKS, FF_CHUNK, w.shape[1])


def _resident(shape):
    return pl.BlockSpec(shape, lambda *_: (0,) * len(shape), pipeline_mode=pl.Buffered(1))


def _params(n_axes):
    return pltpu.CompilerParams(dimension_semantics=("arbitrary",) * n_axes,
                                vmem_limit_bytes=VMEM_LIMIT)


def _ffn_weight_specs():
    return [_resident((N_FF_CHUNKS, D_MODEL, FF_CHUNK)),
            _resident((N_FF_CHUNKS, D_MODEL, FF_CHUNK)),
            _resident((N_FF_CHUNKS, FF_CHUNK, D_MODEL))]


def _ffn1_call(x2d, wg, wu, wd, g, b, w_in):
    n = x2d.shape[0]
    row_spec = pl.BlockSpec((FFN_ROWS, D_MODEL), lambda i: (i, 0))
    vec_spec = _resident((1, D_MODEL))
    return pl.pallas_call(
        _ffn1_kernel,
        grid=(n // FFN_ROWS,),
        in_specs=[row_spec, *_ffn_weight_specs(), vec_spec, vec_spec,
                  _resident((D_MODEL, IN_COLS))],
        out_specs=[row_spec, pl.BlockSpec((FFN_ROWS, IN_COLS), lambda i: (i, 0))],
        out_shape=[jax.ShapeDtypeStruct((n, D_MODEL), F32),
                   jax.ShapeDtypeStruct((n, IN_COLS), BF16)],
        scratch_shapes=[pltpu.VMEM((FFN_ROWS, D_MODEL), BF16),
                        pltpu.VMEM((FFN_ROWS, D_MODEL), F32)],
        compiler_params=_params(1),
        name="ffn1_ln1_qkv",
    )(x2d, wg, wu, wd, g, b, w_in)


def _ffn2_call(x1, na_out, df_out, w_out, g2, b2, wg, wu, wd, g3, b3):
    n = x1.shape[0]
    row_spec = pl.BlockSpec((FFN_ROWS, D_MODEL), lambda i: (i, 0))
    half_spec = pl.BlockSpec((FFN_ROWS, NA_WIDTH), lambda i: (i, 0))
    vec_spec = _resident((1, D_MODEL))
    return pl.pallas_call(
        _ffn2_kernel,
        grid=(n // FFN_ROWS,),
        in_specs=[row_spec, half_spec, half_spec, _resident((D_MODEL, D_MODEL)),
                  vec_spec, vec_spec, *_ffn_weight_specs(), vec_spec, vec_spec],
        out_specs=row_spec,
        out_shape=jax.ShapeDtypeStruct((n, D_MODEL), F32),
        scratch_shapes=[pltpu.VMEM((FFN_ROWS, D_MODEL), F32),
                        pltpu.VMEM((FFN_ROWS, D_MODEL), BF16),
                        pltpu.VMEM((FFN_ROWS, D_MODEL), F32)],
        compiler_params=_params(1),
        name="outproj_ln2_ffn2_ln3",
    )(x1, na_out, df_out, w_out, g2, b2, wg, wu, wd, g3, b3)


def _na_call(p, tbl, batch, seq):
    step_rows = NA_ROWS_PER_STEP * GRID_W
    steps = seq // step_rows
    q_spec = pl.BlockSpec((step_rows, NA_WIDTH), lambda b, i: (b * steps + i, 0))
    return pl.pallas_call(
        _na_kernel,
        grid=(batch, steps),
        in_specs=[q_spec,
                  pl.BlockSpec((seq, NA_WIDTH), lambda b, i: (b, 1)),
                  pl.BlockSpec((seq, NA_WIDTH), lambda b, i: (b, 2)),
                  _resident(tbl.shape)],
        out_specs=q_spec,
        out_shape=jax.ShapeDtypeStruct((batch * seq, NA_WIDTH), BF16),
        compiler_params=_params(2),
        name="neighbourhood_attention",
    )(p, p, p, tbl)


def _diff_call(p, slopes, lq1, lk1, lq2, lk2, subln_g, batch, seq):
    steps = seq // DF_TQ
    q_col0 = 3 * NA_WIDTH // LANES
    k_col0 = q_col0 + DF_WIDTH // LANES
    v_col0 = k_col0 + DF_WIDTH // LANES
    lam_spec = _resident((1, DF_DH))
    return pl.pallas_call(
        _diff_kernel,
        grid=(batch, DF_HEADS, steps),
        in_specs=[pl.BlockSpec(memory_space=pltpu.SMEM),
                  pl.BlockSpec((DF_TQ, LANES), lambda b, h, i: (b * steps + i, q_col0 + h)),
                  pl.BlockSpec((seq, LANES), lambda b, h, i: (b, k_col0 + h)),
                  pl.BlockSpec((seq, LANES), lambda b, h, i: (b, v_col0 + h)),
                  lam_spec, lam_spec, lam_spec, lam_spec,
                  _resident((1, 2 * DF_DH))],
        out_specs=pl.BlockSpec((DF_TQ, LANES), lambda b, h, i: (b * steps + i, h)),
        out_shape=jax.ShapeDtypeStruct((batch * seq, DF_WIDTH), BF16),
        scratch_shapes=[pltpu.VMEM((2 * DF_TQ, LANES), BF16),
                        pltpu.VMEM((3, DF_TQ, DF_TK), F32)],
        compiler_params=_params(3),
        name="differential_attention",
    )(slopes, p, p, p, lq1, lk1, lq2, lk2, subln_g)


def kernel(x, ln1_g, ln1_b, ffn1_w_gate, ffn1_w_up, ffn1_w_down, w_in, na_rpb,
           diff_lambda_q1, diff_lambda_k1, diff_lambda_q2, diff_lambda_k2, diff_subln_g,
           w_out, ln2_g, ln2_b, ffn2_w_gate, ffn2_w_up, ffn2_w_down, ln3_g, ln3_b):
    batch, seq, d = x.shape
    assert d == D_MODEL and seq % (NA_ROWS_PER_STEP * GRID_W) == 0
    assert seq % DF_TK == 0 and DF_TK % DF_TQ == 0
    assert ln1_g.shape[0] == DEPTH == 1
    x2d = x.reshape(batch * seq, d)
    x1, p = _ffn1_call(x2d, _chunk_cols(ffn1_w_gate[0]), _chunk_cols(ffn1_w_up[0]),
                       _chunk_rows(ffn1_w_down[0]), ln1_g, ln1_b, w_in[0].astype(BF16))
    na_out = _na_call(p, _na_bias_table(na_rpb[0]), batch, seq)
    slopes = jnp.exp2(-8.0 * jnp.arange(1, DF_HEADS + 1, dtype=F32) / DF_HEADS)
    df_out = _diff_call(p, slopes, diff_lambda_q1, diff_lambda_k1, diff_lambda_q2,
                        diff_lambda_k2, diff_subln_g, batch, seq)
    out = _ffn2_call(x1, na_out, df_out, w_out[0].astype(BF16), ln2_g, ln2_b,
                     _chunk_cols(ffn2_w_gate[0]), _chunk_cols(ffn2_w_up[0]),
                     _chunk_rows(ffn2_w_down[0]), ln3_g, ln3_b)
    return out.reshape(batch, seq, d)
```

```python
import functools
import math

import jax
import jax.numpy as jnp
from jax import lax
from jax.experimental import pallas as pl
from jax.experimental.pallas import tpu as pltpu

F32 = jnp.float32
BF16 = jnp.bfloat16

D_MODEL = 1024
D_FF = 2816
DEPTH = 1
GRID_W = 64
NA_HEADS, NA_DH = 8, 64
NA_ROWS, NA_COLS = 8, 16
DF_HEADS, DF_DH = 4, 64
NA_WIDTH = NA_HEADS * NA_DH
DF_WIDTH = DF_HEADS * 2 * DF_DH
IN_COLS = 3 * NA_WIDTH + 3 * DF_WIDTH
ALPHA = (2.0 * DEPTH) ** 0.25
LN_EPS = 1e-5
LAM_INIT = 0.8 - 0.6 * math.exp(-0.3 * 0)
NEG_BIG = -1e30
LOG2E = math.log2(math.e)

LANES = 128
MXU_DIM = 256
FF_CHUNK = MXU_DIM
N_FF_CHUNKS = D_FF // FF_CHUNK
FFN_ROWS = 512
NA_ROWS_PER_STEP = 8
DF_TQ = 256
DF_TK = 512
VMEM_LIMIT = 56 * 1024 * 1024


def _layer_norm(y, g, b):
    mu = jnp.mean(y, axis=-1, keepdims=True)
    yc = y - mu
    var = jnp.mean(yc * yc, axis=-1, keepdims=True)
    return yc * lax.rsqrt(var + LN_EPS) * g + b


def _swiglu_into(xb_ref, wg_ref, wu_ref, wd_ref, acc_ref):
    def hidden(c):
        xb = xb_ref[...]
        cols = slice(c * FF_CHUNK, (c + 1) * FF_CHUNK)
        g = jnp.dot(xb, wg_ref[:, cols], preferred_element_type=F32)
        u = jnp.dot(xb, wu_ref[:, cols], preferred_element_type=F32)
        return (g * jax.nn.sigmoid(g) * u).astype(BF16)

    h_prev = hidden(0)
    for c in range(1, N_FF_CHUNKS + 1):
        h_next = hidden(c) if c < N_FF_CHUNKS else None
        d = jnp.dot(h_prev, wd_ref[(c - 1) * FF_CHUNK:c * FF_CHUNK, :],
                    preferred_element_type=F32)
        if c == 1:
            acc_ref[...] = d
        else:
            acc_ref[...] += d
        h_prev = h_next


def _ffn1_kernel(x_ref, wg_ref, wu_ref, wd_ref, g_ref, b_ref, win_ref,
                 x1_ref, p_ref, xb_ref, acc_ref):
    xb_ref[...] = x_ref[...].astype(BF16)
    _swiglu_into(xb_ref, wg_ref, wu_ref, wd_ref, acc_ref)
    x1 = _layer_norm(ALPHA * x_ref[...] + 0.5 * acc_ref[...], g_ref[...], b_ref[...])
    x1_ref[...] = x1
    xb_ref[...] = x1.astype(BF16)
    group_scale = (NA_DH ** -0.5 * LOG2E, None, None, DF_DH ** -0.5 * LOG2E, None, None)
    for j, gs in enumerate(group_scale):
        cols = slice(j * NA_WIDTH, (j + 1) * NA_WIDTH)
        pj = jnp.dot(xb_ref[...], win_ref[:, cols], preferred_element_type=F32)
        if gs is not None:
            pj = pj * gs
        p_ref[:, cols] = pj.astype(BF16)


def _ffn2_kernel(x1_ref, na_ref, df_ref, wout_ref, g2_ref, b2_ref,
                 wg_ref, wu_ref, wd_ref, g3_ref, b3_ref,
                 o_ref, x2_ref, xb_ref, acc_ref):
    mix = jnp.dot(na_ref[...], wout_ref[:NA_WIDTH, :], preferred_element_type=F32)
    mix += jnp.dot(df_ref[...], wout_ref[NA_WIDTH:, :], preferred_element_type=F32)
    x2 = _layer_norm(ALPHA * x1_ref[...] + mix, g2_ref[...], b2_ref[...])
    x2_ref[...] = x2
    xb_ref[...] = x2.astype(BF16)
    _swiglu_into(xb_ref, wg_ref, wu_ref, wd_ref, acc_ref)
    o_ref[...] = _layer_norm(ALPHA * x2_ref[...] + 0.5 * acc_ref[...],
                             g3_ref[...], b3_ref[...])


def _na_kernel(q_ref, k_ref, v_ref, tbl_ref, o_ref):
    rows_total = k_ref.shape[0] // GRID_W
    win_keys = NA_ROWS * GRID_W
    lane = lax.broadcasted_iota(jnp.int32, (GRID_W, LANES), 1)
    lo = lane < NA_DH

    def window(rr):
        r = pl.program_id(1) * NA_ROWS_PER_STEP + rr
        r0 = jnp.clip(r - NA_ROWS // 2, 0, rows_total - NA_ROWS)
        return pl.ds(pl.multiple_of(r0 * GRID_W, GRID_W), win_keys), r - r0

    def scores(rr):
        krows, e = window(rr)
        out = []
        for hp in range(NA_HEADS // 2):
            cols = slice(hp * LANES, (hp + 1) * LANES)
            q = q_ref[rr * GRID_W:(rr + 1) * GRID_W, cols]
            zero = jnp.zeros_like(q)
            qz = jnp.concatenate([jnp.where(lo, q, zero), jnp.where(lo, zero, q)], axis=0)
            s = lax.dot_general(qz, k_ref[krows, cols], (((1,), (1,)), ((), ())),
                                preferred_element_type=F32)
            out.append(s + tbl_ref[hp, e])
        return out

    def finish(rr, s_list):
        krows, _ = window(rr)
        for hp, s in enumerate(s_list):
            cols = slice(hp * LANES, (hp + 1) * LANES)
            m = jnp.max(s, axis=-1, keepdims=True)
            p = jnp.exp2(s - m)
            l = jnp.sum(p, axis=-1, keepdims=True)
            o = jnp.dot(p.astype(BF16), v_ref[krows, cols], preferred_element_type=F32)
            o = o / l
            o_ref[rr * GRID_W:(rr + 1) * GRID_W, cols] = jnp.where(
                lo, o[:GRID_W], o[GRID_W:]).astype(o_ref.dtype)

    pending = scores(0)
    for rr in range(NA_ROWS_PER_STEP):
        upcoming = scores(rr + 1) if rr + 1 < NA_ROWS_PER_STEP else None
        finish(rr, pending)
        pending = upcoming


def _diff_kernel(slope_ref, q_ref, k_ref, v_ref, lq1_ref, lk1_ref, lq2_ref, lk2_ref, g_ref,
                 o_ref, qz_ref, alibi_ref, t_ref):
    tq = q_ref.shape[0]
    n_chunks = k_ref.shape[0] // DF_TK
    groups = DF_TK // LANES
    h = pl.program_id(1)
    i = pl.program_id(2)
    lane = lax.broadcasted_iota(jnp.int32, (tq, LANES), 1)
    q = q_ref[...]
    zero = jnp.zeros_like(q)
    qz_ref[0] = jnp.where(lane < DF_DH, q, zero)
    qz_ref[1] = jnp.where(lane < DF_DH, zero, q)

    a = slope_ref[h] * LOG2E
    rel = (lax.broadcasted_iota(jnp.int32, (tq, DF_TK), 0)
           - lax.broadcasted_iota(jnp.int32, (tq, DF_TK), 1)).astype(F32)
    j_diag = (i * tq) // DF_TK
    alibi_ref[0] = -a * rel
    alibi_ref[1] = a * rel
    alibi_ref[2] = -a * jnp.abs(rel + (i * tq - j_diag * DF_TK).astype(F32))

    kinds, consts = [], []
    for j in range(n_chunks):
        kinds.append(jnp.where(j < j_diag, 0, jnp.where(j > j_diag, 1, 2)))
        off = (i * tq - j * DF_TK).astype(F32)
        consts.append(jnp.where(j == j_diag, 0.0, -a * jnp.abs(off)))

    def lane_groups(x):
        return [x[:, g * LANES:(g + 1) * LANES] for g in range(groups)]

    def score_chunk(half, j, mx):
        cols = slice(j * DF_TK, (j + 1) * DF_TK)
        s = lax.dot_general(qz_ref[half], k_ref[cols, :], (((1,), (1,)), ((), ())),
                            preferred_element_type=F32)
        t = s + alibi_ref[kinds[j]]
        t_ref[half, :, cols] = t
        tg = lane_groups(t)
        while len(tg) > 1:
            tg = [jnp.maximum(x, y) for x, y in zip(tg[::2], tg[1::2])]
        return jnp.maximum(mx, tg[0] + consts[j])

    def value_chunk(half, j, m, lsum, acc):
        cols = slice(j * DF_TK, (j + 1) * DF_TK)
        mj = m - consts[j]
        pg = [jnp.exp2(x - mj) for x in lane_groups(t_ref[half, :, cols])]
        for x in pg:
            lsum = lsum + x
        p = jnp.concatenate([x.astype(BF16) for x in pg], axis=1)
        return lsum, acc + jnp.dot(p, v_ref[cols, :], preferred_element_type=F32)

    def row_max(mx):
        return jnp.broadcast_to(jnp.max(mx, axis=-1, keepdims=True), (tq, LANES))

    neg = jnp.full((tq, LANES), NEG_BIG, F32)
    zeros = jnp.zeros((tq, LANES), F32)
    mx0 = neg
    for j in range(n_chunks):
        mx0 = score_chunk(0, j, mx0)
    m0 = row_max(mx0)
    mx1, l0, acc0 = neg, zeros, zeros
    for j in range(n_chunks):
        mx1 = score_chunk(1, j, mx1)
        l0, acc0 = value_chunk(0, j, m0, l0, acc0)
    m1 = row_max(mx1)
    l1, acc1 = zeros, zeros
    for j in range(n_chunks):
        l1, acc1 = value_chunk(1, j, m1, l1, acc1)
    outs = [acc0 / jnp.sum(l0, axis=-1, keepdims=True),
            acc1 / jnp.sum(l1, axis=-1, keepdims=True)]

    lam = (jnp.exp(jnp.sum(lq1_ref[...] * lk1_ref[...], axis=-1, keepdims=True))
           - jnp.exp(jnp.sum(lq2_ref[...] * lk2_ref[...], axis=-1, keepdims=True))
           + LAM_INIT)
    o = outs[0] - lam * outs[1]
    y = o * lax.rsqrt(jnp.mean(o * o, axis=-1, keepdims=True) + LN_EPS)
    o_ref[...] = (y * g_ref[...] * (1.0 - LAM_INIT)).astype(o_ref.dtype)


def _na_table_kernel(rpb_ref, o_ref):
    pair = pl.program_id(0)
    e = pl.program_id(1)
    n_dr, n_dc = 2 * NA_ROWS - 1, 2 * NA_COLS - 1
    c = lax.broadcasted_iota(jnp.int32, (GRID_W, LANES), 0)
    lane = lax.broadcasted_iota(jnp.int32, (GRID_W, LANES), 1)
    cc = lane & (GRID_W - 1)
    c0 = jnp.clip(c - NA_COLS // 2, 0, GRID_W - NA_COLS)
    in_window = (cc >= c0) & (cc < c0 + NA_COLS)
    dc = jnp.where(in_window, cc - c + (NA_COLS - 1), -1)
    upper = lane >= GRID_W
    for half in range(2):
        head = 2 * pair + half
        for jj in range(NA_ROWS // 2):
            base = (head * n_dr + (2 * jj - e + NA_ROWS - 1)) * n_dc
            acc = jnp.full((GRID_W, LANES), NEG_BIG, F32)
            for x in range(n_dc):
                val = jnp.where(upper, rpb_ref[base + n_dc + x], rpb_ref[base + x])
                acc = jnp.where(dc == x, val, acc)
            o_ref[0, 0, half * GRID_W:(half + 1) * GRID_W, jj * LANES:(jj + 1) * LANES] = (
                acc * LOG2E)


def _na_bias_table(rpb):
    tile = (2 * GRID_W, NA_ROWS * GRID_W)
    return pl.pallas_call(
        _na_table_kernel,
        grid=(NA_HEADS // 2, NA_ROWS),
        in_specs=[pl.BlockSpec(memory_space=pltpu.SMEM)],
        out_specs=pl.BlockSpec((1, 1, *tile), lambda a, e: (a, e, 0, 0)),
        out_shape=jax.ShapeDtypeStruct((NA_HEADS // 2, NA_ROWS, *tile), F32),
        compiler_params=_params(2),
        name="na_bias_table",
    )(rpb.reshape(-1))


def _resident(shape):
    return pl.BlockSpec(shape, lambda *_: (0,) * len(shape), pipeline_mode=pl.Buffered(1))


def _params(n_axes):
    return pltpu.CompilerParams(dimension_semantics=("arbitrary",) * n_axes,
                                vmem_limit_bytes=VMEM_LIMIT)


def _ffn_weight_specs():
    return [_resident((D_MODEL, D_FF)), _resident((D_MODEL, D_FF)), _resident((D_FF, D_MODEL))]


def _ffn1_call(x2d, wg, wu, wd, g, b, w_in):
    n = x2d.shape[0]
    row_spec = pl.BlockSpec((FFN_ROWS, D_MODEL), lambda i: (i, 0))
    vec_spec = _resident((1, D_MODEL))
    return pl.pallas_call(
        _ffn1_kernel,
        grid=(n // FFN_ROWS,),
        in_specs=[row_spec, *_ffn_weight_specs(), vec_spec, vec_spec,
                  _resident((D_MODEL, IN_COLS))],
        out_specs=[row_spec, pl.BlockSpec((FFN_ROWS, IN_COLS), lambda i: (i, 0))],
        out_shape=[jax.ShapeDtypeStruct((n, D_MODEL), F32),
                   jax.ShapeDtypeStruct((n, IN_COLS), BF16)],
        scratch_shapes=[pltpu.VMEM((FFN_ROWS, D_MODEL), BF16),
                        pltpu.VMEM((FFN_ROWS, D_MODEL), F32)],
        compiler_params=_params(1),
        name="ffn1_ln1_qkv",
    )(x2d, wg, wu, wd, g, b, w_in)


def _ffn2_call(x1, na_out, df_out, w_out, g2, b2, wg, wu, wd, g3, b3):
    n = x1.shape[0]
    row_spec = pl.BlockSpec((FFN_ROWS, D_MODEL), lambda i: (i, 0))
    half_spec = pl.BlockSpec((FFN_ROWS, NA_WIDTH), lambda i: (i, 0))
    vec_spec = _resident((1, D_MODEL))
    return pl.pallas_call(
        _ffn2_kernel,
        grid=(n // FFN_ROWS,),
        in_specs=[row_spec, half_spec, half_spec, _resident((D_MODEL, D_MODEL)),
                  vec_spec, vec_spec, *_ffn_weight_specs(), vec_spec, vec_spec],
        out_specs=row_spec,
        out_shape=jax.ShapeDtypeStruct((n, D_MODEL), F32),
        scratch_shapes=[pltpu.VMEM((FFN_ROWS, D_MODEL), F32),
                        pltpu.VMEM((FFN_ROWS, D_MODEL), BF16),
                        pltpu.VMEM((FFN_ROWS, D_MODEL), F32)],
        compiler_params=_params(1),
        name="outproj_ln2_ffn2_ln3",
    )(x1, na_out, df_out, w_out, g2, b2, wg, wu, wd, g3, b3)


def _na_call(p, tbl, batch, seq):
    step_rows = NA_ROWS_PER_STEP * GRID_W
    steps = seq // step_rows
    q_spec = pl.BlockSpec((step_rows, NA_WIDTH), lambda b, i: (b * steps + i, 0))
    return pl.pallas_call(
        _na_kernel,
        grid=(batch, steps),
        in_specs=[q_spec,
                  pl.BlockSpec((seq, NA_WIDTH), lambda b, i: (b, 1)),
                  pl.BlockSpec((seq, NA_WIDTH), lambda b, i: (b, 2)),
                  _resident(tbl.shape)],
        out_specs=q_spec,
        out_shape=jax.ShapeDtypeStruct((batch * seq, NA_WIDTH), BF16),
        compiler_params=_params(2),
        name="neighbourhood_attention",
    )(p, p, p, tbl)


def _diff_call(p, slopes, lq1, lk1, lq2, lk2, subln_g, batch, seq):
    steps = seq // DF_TQ
    q_col0 = 3 * NA_WIDTH // LANES
    k_col0 = q_col0 + DF_WIDTH // LANES
    v_col0 = k_col0 + DF_WIDTH // LANES
    lam_spec = _resident((1, DF_DH))
    return pl.pallas_call(
        _diff_kernel,
        grid=(batch, DF_HEADS, steps),
        in_specs=[pl.BlockSpec(memory_space=pltpu.SMEM),
                  pl.BlockSpec((DF_TQ, LANES), lambda b, h, i: (b * steps + i, q_col0 + h)),
                  pl.BlockSpec((seq, LANES), lambda b, h, i: (b, k_col0 + h)),
                  pl.BlockSpec((seq, LANES), lambda b, h, i: (b, v_col0 + h)),
                  lam_spec, lam_spec, lam_spec, lam_spec,
                  _resident((1, 2 * DF_DH))],
        out_specs=pl.BlockSpec((DF_TQ, LANES), lambda b, h, i: (b * steps + i, h)),
        out_shape=jax.ShapeDtypeStruct((batch * seq, DF_WIDTH), BF16),
        scratch_shapes=[pltpu.VMEM((2, DF_TQ, LANES), BF16),
                        pltpu.VMEM((3, DF_TQ, DF_TK), F32),
                        pltpu.VMEM((2, DF_TQ, seq), F32)],
        compiler_params=_params(3),
        name="differential_attention",
    )(slopes, p, p, p, lq1, lk1, lq2, lk2, subln_g)


def kernel(x, ln1_g, ln1_b, ffn1_w_gate, ffn1_w_up, ffn1_w_down, w_in, na_rpb,
           diff_lambda_q1, diff_lambda_k1, diff_lambda_q2, diff_lambda_k2, diff_subln_g,
           w_out, ln2_g, ln2_b, ffn2_w_gate, ffn2_w_up, ffn2_w_down, ln3_g, ln3_b):
    batch, seq, d = x.shape
    assert d == D_MODEL and seq % (NA_ROWS_PER_STEP * GRID_W) == 0
    assert seq % DF_TK == 0 and DF_TK % DF_TQ == 0
    assert ln1_g.shape[0] == DEPTH == 1
    x2d = x.reshape(batch * seq, d)
    x1, p = _ffn1_call(x2d, ffn1_w_gate[0].astype(BF16), ffn1_w_up[0].astype(BF16),
                       ffn1_w_down[0].astype(BF16), ln1_g, ln1_b, w_in[0].astype(BF16))
    na_out = _na_call(p, _na_bias_table(na_rpb[0]), batch, seq)
    slopes = jnp.exp2(-8.0 * jnp.arange(1, DF_HEADS + 1, dtype=F32) / DF_HEADS)
    df_out = _diff_call(p, slopes, diff_lambda_q1, diff_lambda_k1, diff_lambda_q2,
                        diff_lambda_k2, diff_subln_g, batch, seq)
    out = _ffn2_call(x1, na_out, df_out, w_out[0].astype(BF16), ln2_g, ln2_b,
                     ffn2_w_gate[0].astype(BF16), ffn2_w_up[0].astype(BF16),
                     ffn2_w_down[0].astype(BF16), ln3_g, ln3_b)
    return out.reshape(batch, seq, d)
```

```python
import functools
import math

import jax
import jax.numpy as jnp
from jax import lax
from jax.experimental import pallas as pl
from jax.experimental.pallas import tpu as pltpu

F32 = jnp.float32
BF16 = jnp.bfloat16

D_MODEL = 1024
D_FF = 2816
DEPTH = 1
GRID_W = 64
NA_HEADS, NA_DH = 8, 64
NA_ROWS, NA_COLS = 8, 16
DF_HEADS, DF_DH = 4, 64
NA_WIDTH = NA_HEADS * NA_DH
DF_WIDTH = DF_HEADS * 2 * DF_DH
IN_COLS = 3 * NA_WIDTH + 3 * DF_WIDTH
ALPHA = (2.0 * DEPTH) ** 0.25
LN_EPS = 1e-5
LAM_INIT = 0.8 - 0.6 * math.exp(-0.3 * 0)
NEG_BIG = -1e30
LOG2E = math.log2(math.e)

LANES = 128
MXU_DIM = 256
FF_CHUNK = MXU_DIM
N_FF_CHUNKS = D_FF // FF_CHUNK
FFN_ROWS = 512
NA_ROWS_PER_STEP = 8
DF_TQ = 256
DF_TK = 512
DF_BLOCKS = 4
VMEM_LIMIT = 56 * 1024 * 1024


def _layer_norm(y, g, b):
    mu = jnp.mean(y, axis=-1, keepdims=True)
    yc = y - mu
    var = jnp.mean(yc * yc, axis=-1, keepdims=True)
    return yc * lax.rsqrt(var + LN_EPS) * g + b


def _swiglu_into(xb_ref, wg_ref, wu_ref, wd_ref, acc_ref):
    def hidden(c):
        xb = xb_ref[...]
        cols = slice(c * FF_CHUNK, (c + 1) * FF_CHUNK)
        g = jnp.dot(xb, wg_ref[:, cols], preferred_element_type=F32)
        u = jnp.dot(xb, wu_ref[:, cols], preferred_element_type=F32)
        return (g * jax.nn.sigmoid(g) * u).astype(BF16)

    h_prev = hidden(0)
    for c in range(1, N_FF_CHUNKS + 1):
        h_next = hidden(c) if c < N_FF_CHUNKS else None
        d = jnp.dot(h_prev, wd_ref[(c - 1) * FF_CHUNK:c * FF_CHUNK, :],
                    preferred_element_type=F32)
        if c == 1:
            acc_ref[...] = d
        else:
            acc_ref[...] += d
        h_prev = h_next


def _ffn1_kernel(x_ref, wg_ref, wu_ref, wd_ref, g_ref, b_ref, win_ref,
                 x1_ref, p_ref, xb_ref, acc_ref):
    xb_ref[...] = x_ref[...].astype(BF16)
    _swiglu_into(xb_ref, wg_ref, wu_ref, wd_ref, acc_ref)
    x1 = _layer_norm(ALPHA * x_ref[...] + 0.5 * acc_ref[...], g_ref[...], b_ref[...])
    x1_ref[...] = x1
    xb_ref[...] = x1.astype(BF16)
    group_scale = (NA_DH ** -0.5 * LOG2E, None, None, DF_DH ** -0.5 * LOG2E, None, None)
    for j, gs in enumerate(group_scale):
        cols = slice(j * NA_WIDTH, (j + 1) * NA_WIDTH)
        pj = jnp.dot(xb_ref[...], win_ref[:, cols], preferred_element_type=F32)
        if gs is not None:
            pj = pj * gs
        p_ref[:, cols] = pj.astype(BF16)


def _ffn2_kernel(x1_ref, na_ref, df_ref, wout_ref, g2_ref, b2_ref,
                 wg_ref, wu_ref, wd_ref, g3_ref, b3_ref,
                 o_ref, x2_ref, xb_ref, acc_ref):
    mix = jnp.dot(na_ref[...], wout_ref[:NA_WIDTH, :], preferred_element_type=F32)
    mix += jnp.dot(df_ref[...], wout_ref[NA_WIDTH:, :], preferred_element_type=F32)
    x2 = _layer_norm(ALPHA * x1_ref[...] + mix, g2_ref[...], b2_ref[...])
    x2_ref[...] = x2
    xb_ref[...] = x2.astype(BF16)
    _swiglu_into(xb_ref, wg_ref, wu_ref, wd_ref, acc_ref)
    o_ref[...] = _layer_norm(ALPHA * x2_ref[...] + 0.5 * acc_ref[...],
                             g3_ref[...], b3_ref[...])


def _na_kernel(q_ref, k_ref, v_ref, tbl_ref, o_ref):
    rows_total = k_ref.shape[0] // GRID_W
    win_keys = NA_ROWS * GRID_W
    lane = lax.broadcasted_iota(jnp.int32, (GRID_W, LANES), 1)
    lo = lane < NA_DH

    def window(rr):
        r = pl.program_id(1) * NA_ROWS_PER_STEP + rr
        r0 = jnp.clip(r - NA_ROWS // 2, 0, rows_total - NA_ROWS)
        return pl.ds(pl.multiple_of(r0 * GRID_W, GRID_W), win_keys), r - r0

    def scores(rr):
        krows, e = window(rr)
        out = []
        for hp in range(NA_HEADS // 2):
            cols = slice(hp * LANES, (hp + 1) * LANES)
            q = q_ref[rr * GRID_W:(rr + 1) * GRID_W, cols]
            zero = jnp.zeros_like(q)
            qz = jnp.concatenate([jnp.where(lo, q, zero), jnp.where(lo, zero, q)], axis=0)
            s = lax.dot_general(qz, k_ref[krows, cols], (((1,), (1,)), ((), ())),
                                preferred_element_type=F32)
            out.append(s + tbl_ref[hp, e])
        return out

    def finish(rr, s_list):
        krows, _ = window(rr)
        for hp, s in enumerate(s_list):
            cols = slice(hp * LANES, (hp + 1) * LANES)
            m = jnp.max(s, axis=-1, keepdims=True)
            p = jnp.exp2(s - m)
            l = jnp.sum(p, axis=-1, keepdims=True)
            o = jnp.dot(p.astype(BF16), v_ref[krows, cols], preferred_element_type=F32)
            o = o / l
            o_ref[rr * GRID_W:(rr + 1) * GRID_W, cols] = jnp.where(
                lo, o[:GRID_W], o[GRID_W:]).astype(o_ref.dtype)

    pending = scores(0)
    for rr in range(NA_ROWS_PER_STEP):
        upcoming = scores(rr + 1) if rr + 1 < NA_ROWS_PER_STEP else None
        finish(rr, pending)
        pending = upcoming


def _split3(x):
    hi = x.astype(BF16).astype(F32)
    mid = (x - hi).astype(BF16).astype(F32)
    return hi, mid, (x - hi - mid).astype(BF16).astype(F32)


def _diff_kernel(slope_ref, q_ref, k_ref, v_ref, lq1_ref, lk1_ref, lq2_ref, lk2_ref, g_ref,
                 o_ref, kp_ref, qv_ref, diag_ref, t_ref):
    tq, tk = DF_TQ, DF_TK
    n_chunks = k_ref.shape[0] // tk
    ratio = tk // tq
    groups = tk // LANES
    h = pl.program_id(1)
    i = pl.program_id(2)
    a = slope_ref[h] * LOG2E
    lane = lax.broadcasted_iota(jnp.int32, (tq, LANES), 1)

    @pl.when(i == 0)
    def _per_head_setup():
        t_keys = k_ref.shape[0]
        kp_ref[:, :LANES] = k_ref[...]
        klane = lax.broadcasted_iota(jnp.int32, (t_keys, LANES), 1)
        kpos = lax.broadcasted_iota(jnp.int32, (t_keys, LANES), 0).astype(F32)
        hi, mid, lo = _split3(a * kpos)
        feat = jnp.where(klane == 3, hi, jnp.where(klane == 4, mid, jnp.where(
            klane == 5, lo, 0.0)))
        kp_ref[:, LANES:] = jnp.where(klane < 3, 1.0, feat).astype(BF16)
        rel = (lax.broadcasted_iota(jnp.int32, (tq, tk), 0)
               - lax.broadcasted_iota(jnp.int32, (tq, tk), 1)).astype(F32)
        for par in range(ratio):
            diag_ref[par] = -a * jnp.abs(rel + float(par * tq))

    block_info = []
    for blk in range(DF_BLOCKS):
        ib = i * DF_BLOCKS + blk
        q = q_ref[blk * tq:(blk + 1) * tq, :]
        zero = jnp.zeros_like(q)
        qpos = (ib * tq + lax.broadcasted_iota(jnp.int32, (tq, LANES), 0)).astype(F32)
        hi, mid, lo = _split3(-a * qpos)
        before = jnp.where(lane == 0, hi, jnp.where(lane == 1, mid, jnp.where(
            lane == 2, lo, jnp.where(lane < 6, 1.0, 0.0)))).astype(BF16)
        for half in range(2):
            qz = jnp.where(lane < DF_DH, q, zero) if half == 0 else jnp.where(
                lane < DF_DH, zero, q)
            for kind, feat in enumerate((before, -before, zero)):
                qv_ref[blk, half, kind, :, :LANES] = qz
                qv_ref[blk, half, kind, :, LANES:] = feat
        block_info.append((lax.div(ib, ratio), lax.rem(ib, ratio)))

    def chunk_rows(blk, d):
        j_diag = block_info[blk][0]
        j = lax.rem(j_diag + d, n_chunks)
        return j, pl.ds(pl.multiple_of(j * tk, tk), tk)

    def lane_groups(x):
        return [x[:, g * LANES:(g + 1) * LANES] for g in range(groups)]

    def score_chunk(n, d, mx):
        blk, half = divmod(n, 2)
        j_diag, par = block_info[blk]
        j, rows = chunk_rows(blk, d)
        dims = (((1,), (1,)), ((), ()))
        if d == 0:
            t = lax.dot_general(qv_ref[blk, half, 2], kp_ref[rows, :], dims,
                                preferred_element_type=F32) + diag_ref[par]
        else:
            kind = jnp.where(j < j_diag, 0, 1)
            t = lax.dot_general(qv_ref[blk, half, kind], kp_ref[rows, :], dims,
                                preferred_element_type=F32)
        t_ref[n % 2, :, d * tk:(d + 1) * tk] = t
        tg = lane_groups(t)
        while len(tg) > 1:
            tg = [jnp.maximum(x, y) for x, y in zip(tg[::2], tg[1::2])]
        return jnp.maximum(mx, tg[0])

    def value_chunk(n, d, m, lsum, acc):
        _, rows = chunk_rows(n // 2, d)
        pg = [jnp.exp2(x - m) for x in lane_groups(t_ref[n % 2, :, d * tk:(d + 1) * tk])]
        for x in pg:
            lsum = lsum + x
        p = jnp.concatenate([x.astype(BF16) for x in pg], axis=1)
        return lsum, acc + jnp.dot(p, v_ref[rows, :], preferred_element_type=F32)

    n_sets = 2 * DF_BLOCKS
    neg = jnp.full((tq, LANES), NEG_BIG, F32)
    zeros = jnp.zeros((tq, LANES), F32)
    mx = neg
    for d in range(n_chunks):
        mx = score_chunk(0, d, mx)
    outs = []
    for n in range(n_sets):
        m = jnp.broadcast_to(jnp.max(mx, axis=-1, keepdims=True), (tq, LANES))
        mx, lsum, acc = neg, zeros, zeros
        for d in range(n_chunks):
            if n + 1 < n_sets:
                mx = score_chunk(n + 1, d, mx)
            lsum, acc = value_chunk(n, d, m, lsum, acc)
        outs.append(acc / jnp.sum(lsum, axis=-1, keepdims=True))

    lam = (jnp.exp(jnp.sum(lq1_ref[...] * lk1_ref[...], axis=-1, keepdims=True))
           - jnp.exp(jnp.sum(lq2_ref[...] * lk2_ref[...], axis=-1, keepdims=True))
           + LAM_INIT)
    for blk in range(DF_BLOCKS):
        o = outs[2 * blk] - lam * outs[2 * blk + 1]
        y = o * lax.rsqrt(jnp.mean(o * o, axis=-1, keepdims=True) + LN_EPS)
        o_ref[blk * tq:(blk + 1) * tq, :] = (y * g_ref[...] * (1.0 - LAM_INIT)).astype(
            o_ref.dtype)


def _na_table_kernel(rpb_ref, o_ref):
    pair = pl.program_id(0)
    e = pl.program_id(1)
    n_dr, n_dc = 2 * NA_ROWS - 1, 2 * NA_COLS - 1
    c = lax.broadcasted_iota(jnp.int32, (GRID_W, LANES), 0)
    lane = lax.broadcasted_iota(jnp.int32, (GRID_W, LANES), 1)
    cc = lane & (GRID_W - 1)
    c0 = jnp.clip(c - NA_COLS // 2, 0, GRID_W - NA_COLS)
    in_window = (cc >= c0) & (cc < c0 + NA_COLS)
    dc = jnp.where(in_window, cc - c + (NA_COLS - 1), -1)
    upper = lane >= GRID_W
    for half in range(2):
        head = 2 * pair + half
        for jj in range(NA_ROWS // 2):
            base = (head * n_dr + (2 * jj - e + NA_ROWS - 1)) * n_dc
            acc = jnp.full((GRID_W, LANES), NEG_BIG, F32)
            for x in range(n_dc):
                val = jnp.where(upper, rpb_ref[base + n_dc + x], rpb_ref[base + x])
                acc = jnp.where(dc == x, val, acc)
            o_ref[0, 0, half * GRID_W:(half + 1) * GRID_W, jj * LANES:(jj + 1) * LANES] = (
                acc * LOG2E)


def _na_bias_table(rpb):
    tile = (2 * GRID_W, NA_ROWS * GRID_W)
    return pl.pallas_call(
        _na_table_kernel,
        grid=(NA_HEADS // 2, NA_ROWS),
        in_specs=[pl.BlockSpec(memory_space=pltpu.SMEM)],
        out_specs=pl.BlockSpec((1, 1, *tile), lambda a, e: (a, e, 0, 0)),
        out_shape=jax.ShapeDtypeStruct((NA_HEADS // 2, NA_ROWS, *tile), F32),
        compiler_params=_params(2),
        name="na_bias_table",
    )(rpb.reshape(-1))


def _resident(shape):
    return pl.BlockSpec(shape, lambda *_: (0,) * len(shape), pipeline_mode=pl.Buffered(1))


def _params(n_axes):
    return pltpu.CompilerParams(dimension_semantics=("arbitrary",) * n_axes,
                                vmem_limit_bytes=VMEM_LIMIT)


def _ffn_weight_specs():
    return [_resident((D_MODEL, D_FF)), _resident((D_MODEL, D_FF)), _resident((D_FF, D_MODEL))]


def _ffn1_call(x2d, wg, wu, wd, g, b, w_in):
    n = x2d.shape[0]
    row_spec = pl.BlockSpec((FFN_ROWS, D_MODEL), lambda i: (i, 0))
    vec_spec = _resident((1, D_MODEL))
    return pl.pallas_call(
        _ffn1_kernel,
        grid=(n // FFN_ROWS,),
        in_specs=[row_spec, *_ffn_weight_specs(), vec_spec, vec_spec,
                  _resident((D_MODEL, IN_COLS))],
        out_specs=[row_spec, pl.BlockSpec((FFN_ROWS, IN_COLS), lambda i: (i, 0))],
        out_shape=[jax.ShapeDtypeStruct((n, D_MODEL), F32),
                   jax.ShapeDtypeStruct((n, IN_COLS), BF16)],
        scratch_shapes=[pltpu.VMEM((FFN_ROWS, D_MODEL), BF16),
                        pltpu.VMEM((FFN_ROWS, D_MODEL), F32)],
        compiler_params=_params(1),
        name="ffn1_ln1_qkv",
    )(x2d, wg, wu, wd, g, b, w_in)


def _ffn2_call(x1, na_out, df_out, w_out, g2, b2, wg, wu, wd, g3, b3):
    n = x1.shape[0]
    row_spec = pl.BlockSpec((FFN_ROWS, D_MODEL), lambda i: (i, 0))
    half_spec = pl.BlockSpec((FFN_ROWS, NA_WIDTH), lambda i: (i, 0))
    vec_spec = _resident((1, D_MODEL))
    return pl.pallas_call(
        _ffn2_kernel,
        grid=(n // FFN_ROWS,),
        in_specs=[row_spec, half_spec, half_spec, _resident((D_MODEL, D_MODEL)),
                  vec_spec, vec_spec, *_ffn_weight_specs(), vec_spec, vec_spec],
        out_specs=row_spec,
        out_shape=jax.ShapeDtypeStruct((n, D_MODEL), F32),
        scratch_shapes=[pltpu.VMEM((FFN_ROWS, D_MODEL), F32),
                        pltpu.VMEM((FFN_ROWS, D_MODEL), BF16),
                        pltpu.VMEM((FFN_ROWS, D_MODEL), F32)],
        compiler_params=_params(1),
        name="outproj_ln2_ffn2_ln3",
    )(x1, na_out, df_out, w_out, g2, b2, wg, wu, wd, g3, b3)


def _na_call(p, tbl, batch, seq):
    step_rows = NA_ROWS_PER_STEP * GRID_W
    steps = seq // step_rows
    q_spec = pl.BlockSpec((step_rows, NA_WIDTH), lambda b, i: (b * steps + i, 0))
    return pl.pallas_call(
        _na_kernel,
        grid=(batch, steps),
        in_specs=[q_spec,
                  pl.BlockSpec((seq, NA_WIDTH), lambda b, i: (b, 1)),
                  pl.BlockSpec((seq, NA_WIDTH), lambda b, i: (b, 2)),
                  _resident(tbl.shape)],
        out_specs=q_spec,
        out_shape=jax.ShapeDtypeStruct((batch * seq, NA_WIDTH), BF16),
        compiler_params=_params(2),
        name="neighbourhood_attention",
    )(p, p, p, tbl)


def _diff_call(p, slopes, lq1, lk1, lq2, lk2, subln_g, batch, seq):
    step_rows = DF_BLOCKS * DF_TQ
    steps = seq // step_rows
    q_col0 = 3 * NA_WIDTH // LANES
    k_col0 = q_col0 + DF_WIDTH // LANES
    v_col0 = k_col0 + DF_WIDTH // LANES
    lam_spec = _resident((1, DF_DH))
    return pl.pallas_call(
        _diff_kernel,
        grid=(batch, DF_HEADS, steps),
        in_specs=[pl.BlockSpec(memory_space=pltpu.SMEM),
                  pl.BlockSpec((step_rows, LANES), lambda b, h, i: (b * steps + i, q_col0 + h)),
                  pl.BlockSpec((seq, LANES), lambda b, h, i: (b, k_col0 + h)),
                  pl.BlockSpec((seq, LANES), lambda b, h, i: (b, v_col0 + h)),
                  lam_spec, lam_spec, lam_spec, lam_spec,
                  _resident((1, 2 * DF_DH))],
        out_specs=pl.BlockSpec((step_rows, LANES), lambda b, h, i: (b * steps + i, h)),
        out_shape=jax.ShapeDtypeStruct((batch * seq, DF_WIDTH), BF16),
        scratch_shapes=[pltpu.VMEM((seq, 2 * LANES), BF16),
                        pltpu.VMEM((DF_BLOCKS, 2, 3, DF_TQ, 2 * LANES), BF16),
                        pltpu.VMEM((DF_TK // DF_TQ, DF_TQ, DF_TK), F32),
                        pltpu.VMEM((2, DF_TQ, seq), F32)],
        compiler_params=_params(3),
        name="differential_attention",
    )(slopes, p, p, p, lq1, lk1, lq2, lk2, subln_g)


def kernel(x, ln1_g, ln1_b, ffn1_w_gate, ffn1_w_up, ffn1_w_down, w_in, na_rpb,
           diff_lambda_q1, diff_lambda_k1, diff_lambda_q2, diff_lambda_k2, diff_subln_g,
           w_out, ln2_g, ln2_b, ffn2_w_gate, ffn2_w_up, ffn2_w_down, ln3_g, ln3_b):
    batch, seq, d = x.shape
    assert d == D_MODEL and seq % (NA_ROWS_PER_STEP * GRID_W) == 0
    assert seq % DF_TK == 0 and DF_TK % DF_TQ == 0 and seq % (DF_BLOCKS * DF_TQ) == 0
    assert ln1_g.shape[0] == DEPTH == 1
    x2d = x.reshape(batch * seq, d)
    x1, p = _ffn1_call(x2d, ffn1_w_gate[0].astype(BF16), ffn1_w_up[0].astype(BF16),
                       ffn1_w_down[0].astype(BF16), ln1_g, ln1_b, w_in[0].astype(BF16))
    na_out = _na_call(p, _na_bias_table(na_rpb[0]), batch, seq)
    slopes = jnp.exp2(-8.0 * jnp.arange(1, DF_HEADS + 1, dtype=F32) / DF_HEADS)
    df_out = _diff_call(p, slopes, diff_lambda_q1, diff_lambda_k1, diff_lambda_q2,
                        diff_lambda_k2, diff_subln_g, batch, seq)
    out = _ffn2_call(x1, na_out, df_out, w_out[0].astype(BF16), ln2_g, ln2_b,
                     ffn2_w_gate[0].astype(BF16), ffn2_w_up[0].astype(BF16),
                     ffn2_w_down[0].astype(BF16), ln3_g, ln3_b)
    return out.reshape(batch, seq, d)
```

```python
import functools
import math

import jax
import jax.numpy as jnp
from jax import lax
from jax.experimental import pallas as pl
from jax.experimental.pallas import tpu as pltpu

F32 = jnp.float32
BF16 = jnp.bfloat16

D_MODEL = 1024
D_FF = 2816
DEPTH = 1
GRID_W = 64
NA_HEADS, NA_DH = 8, 64
NA_ROWS, NA_COLS = 8, 16
DF_HEADS, DF_DH = 4, 64
NA_WIDTH = NA_HEADS * NA_DH
DF_WIDTH = DF_HEADS * 2 * DF_DH
IN_COLS = 3 * NA_WIDTH + 3 * DF_WIDTH
ALPHA = (2.0 * DEPTH) ** 0.25
LN_EPS = 1e-5
LAM_INIT = 0.8 - 0.6 * math.exp(-0.3 * 0)
NEG_BIG = -1e30
LOG2E = math.log2(math.e)

LANES = 128
MXU_DIM = 256
FF_CHUNK = MXU_DIM
N_FF_CHUNKS = D_FF // FF_CHUNK
FFN_ROWS = 512
FFN_SUB = 256
FFN_SKEW = 2
CAST_ROWS = 128
NA_ROWS_PER_STEP = 8
DF_TQ = 256
DF_TK = 512
DF_BLOCKS = 4
VMEM_LIMIT = 56 * 1024 * 1024


def _layer_norm(y, g, b):
    mu = jnp.mean(y, axis=-1, keepdims=True)
    yc = y - mu
    var = jnp.mean(yc * yc, axis=-1, keepdims=True)
    return yc * lax.rsqrt(var + LN_EPS) * g + b


def _cast_into(w_hbm, dst_ref, stage_ref, sem_ref):
    n = w_hbm.shape[0] // CAST_ROWS

    def slab_copy(c, slot):
        return pltpu.make_async_copy(w_hbm.at[pl.ds(c * CAST_ROWS, CAST_ROWS), :],
                                     stage_ref.at[slot], sem_ref.at[slot])

    slab_copy(0, 0).start()

    def body(c, carry):
        slot = lax.rem(c, 2)
        slab_copy(c, slot).wait()

        @pl.when(c + 1 < n)
        def _prefetch():
            slab_copy(c + 1, 1 - slot).start()

        rows = pl.ds(pl.multiple_of(c * CAST_ROWS, CAST_ROWS), CAST_ROWS)
        dst_ref[rows, :] = stage_ref[slot].astype(BF16)
        return carry

    lax.fori_loop(0, n, body, 0)


def _swiglu_lanes(xb_ref, wg_ref, wu_ref, wd_ref, acc_ref, before_lane=None, after_lane=None):
    lanes = [slice(r, r + FFN_SUB) for r in range(0, FFN_ROWS, FFN_SUB)]
    pending = [None] * len(lanes)
    if before_lane is not None:
        for rows in lanes:
            before_lane(rows)

    def slot(k, s):
        rows = lanes[k]
        h_new = None
        if s < N_FF_CHUNKS:
            xb = xb_ref[rows, :]
            cols = slice(s * FF_CHUNK, (s + 1) * FF_CHUNK)
            g = jnp.dot(xb, wg_ref[:, cols], preferred_element_type=F32)
            u = jnp.dot(xb, wu_ref[:, cols], preferred_element_type=F32)
            h_new = (g * jax.nn.sigmoid(g) * u).astype(BF16)
        if s >= 1:
            d = jnp.dot(pending[k], wd_ref[(s - 1) * FF_CHUNK:s * FF_CHUNK, :],
                        preferred_element_type=F32)
            if s == 1:
                acc_ref[rows, :] = d
            else:
                acc_ref[rows, :] += d
        pending[k] = h_new

    n_slots = N_FF_CHUNKS + 1
    finished = []
    for t in range(n_slots + FFN_SKEW * (len(lanes) - 1)):
        for k in range(len(lanes)):
            s = t - FFN_SKEW * k
            if 0 <= s < n_slots:
                slot(k, s)
                if s == n_slots - 1:
                    finished.append(k)
    if after_lane is not None:
        for k in finished:
            after_lane(lanes[k])


def _ffn1_kernel(x_ref, wg_hbm, wu_hbm, wd_hbm, g_ref, b_ref, win_hbm,
                 x1_ref, p_ref, xb_ref, acc_ref, wg_ref, wu_ref, wd_ref, win_ref,
                 stage_ff, stage_d, stage_in, sem_ref):
    @pl.when(pl.program_id(0) == 0)
    def _load_weights():
        _cast_into(wg_hbm, wg_ref, stage_ff, sem_ref)
        _cast_into(wu_hbm, wu_ref, stage_ff, sem_ref)
        _cast_into(wd_hbm, wd_ref, stage_d, sem_ref)
        _cast_into(win_hbm, win_ref, stage_in, sem_ref)

    xb_ref[...] = x_ref[...].astype(BF16)

    def project(rows):
        x1 = _layer_norm(ALPHA * x_ref[rows, :] + 0.5 * acc_ref[rows, :],
                         g_ref[...], b_ref[...])
        x1_ref[rows, :] = x1
        x1b = x1.astype(BF16)
        group_scale = (NA_DH ** -0.5 * LOG2E, None, None, DF_DH ** -0.5 * LOG2E, None, None)
        for j, gs in enumerate(group_scale):
            cols = slice(j * NA_WIDTH, (j + 1) * NA_WIDTH)
            pj = jnp.dot(x1b, win_ref[:, cols], preferred_element_type=F32)
            if gs is not None:
                pj = pj * gs
            p_ref[rows, cols] = pj.astype(BF16)

    _swiglu_lanes(xb_ref, wg_ref, wu_ref, wd_ref, acc_ref, after_lane=project)


def _ffn2_kernel(x1_ref, na_ref, df_ref, wout_hbm, g2_ref, b2_ref,
                 wg_hbm, wu_hbm, wd_hbm, g3_ref, b3_ref,
                 o_ref, x2_ref, xb_ref, acc_ref, wg_ref, wu_ref, wd_ref, wout_ref,
                 stage_ff, stage_d, sem_ref):
    @pl.when(pl.program_id(0) == 0)
    def _load_weights():
        _cast_into(wout_hbm, wout_ref, stage_d, sem_ref)
        _cast_into(wg_hbm, wg_ref, stage_ff, sem_ref)
        _cast_into(wu_hbm, wu_ref, stage_ff, sem_ref)
        _cast_into(wd_hbm, wd_ref, stage_d, sem_ref)

    def mix_in(rows):
        mix = jnp.dot(na_ref[rows, :], wout_ref[:NA_WIDTH, :], preferred_element_type=F32)
        mix += jnp.dot(df_ref[rows, :], wout_ref[NA_WIDTH:, :], preferred_element_type=F32)
        x2 = _layer_norm(ALPHA * x1_ref[rows, :] + mix, g2_ref[...], b2_ref[...])
        x2_ref[rows, :] = x2
        xb_ref[rows, :] = x2.astype(BF16)

    def finish(rows):
        o_ref[rows, :] = _layer_norm(ALPHA * x2_ref[rows, :] + 0.5 * acc_ref[rows, :],
                                     g3_ref[...], b3_ref[...])

    _swiglu_lanes(xb_ref, wg_ref, wu_ref, wd_ref, acc_ref, before_lane=mix_in,
                  after_lane=finish)


def _na_kernel(q_ref, k_ref, v_ref, tbl_ref, o_ref):
    rows_total = k_ref.shape[0] // GRID_W
    win_keys = NA_ROWS * GRID_W
    lane = lax.broadcasted_iota(jnp.int32, (GRID_W, LANES), 1)
    lo = lane < NA_DH

    def window(rr):
        r = pl.program_id(1) * NA_ROWS_PER_STEP + rr
        r0 = jnp.clip(r - NA_ROWS // 2, 0, rows_total - NA_ROWS)
        return pl.ds(pl.multiple_of(r0 * GRID_W, GRID_W), win_keys), r - r0

    def scores(rr):
        krows, e = window(rr)
        out = []
        for hp in range(NA_HEADS // 2):
            cols = slice(hp * LANES, (hp + 1) * LANES)
            q = q_ref[rr * GRID_W:(rr + 1) * GRID_W, cols]
            zero = jnp.zeros_like(q)
            qz = jnp.concatenate([jnp.where(lo, q, zero), jnp.where(lo, zero, q)], axis=0)
            s = lax.dot_general(qz, k_ref[krows, cols], (((1,), (1,)), ((), ())),
                                preferred_element_type=F32)
            out.append(s + tbl_ref[hp, e])
        return out

    def finish(rr, s_list):
        krows, _ = window(rr)
        for hp, s in enumerate(s_list):
            cols = slice(hp * LANES, (hp + 1) * LANES)
            m = jnp.max(s, axis=-1, keepdims=True)
            p = jnp.exp2(s - m)
            l = jnp.sum(p, axis=-1, keepdims=True)
            o = jnp.dot(p.astype(BF16), v_ref[krows, cols], preferred_element_type=F32)
            o = o / l
            o_ref[rr * GRID_W:(rr + 1) * GRID_W, cols] = jnp.where(
                lo, o[:GRID_W], o[GRID_W:]).astype(o_ref.dtype)

    pending = scores(0)
    for rr in range(NA_ROWS_PER_STEP):
        upcoming = scores(rr + 1) if rr + 1 < NA_ROWS_PER_STEP else None
        finish(rr, pending)
        pending = upcoming


def _split3(x):
    hi = x.astype(BF16).astype(F32)
    mid = (x - hi).astype(BF16).astype(F32)
    return hi, mid, (x - hi - mid).astype(BF16).astype(F32)


def _diff_kernel(slope_ref, q_ref, k_ref, v_ref, lq1_ref, lk1_ref, lq2_ref, lk2_ref, g_ref,
                 o_ref, kp_ref, qv_ref, diag_ref, t_ref):
    tq, tk = DF_TQ, DF_TK
    n_chunks = k_ref.shape[0] // tk
    ratio = tk // tq
    groups = tk // LANES
    h = pl.program_id(1)
    i = pl.program_id(2)
    a = slope_ref[h] * LOG2E
    lane = lax.broadcasted_iota(jnp.int32, (tq, LANES), 1)

    @pl.when(i == 0)
    def _per_head_setup():
        t_keys = k_ref.shape[0]
        kp_ref[:, :LANES] = k_ref[...]
        klane = lax.broadcasted_iota(jnp.int32, (t_keys, LANES), 1)
        kpos = lax.broadcasted_iota(jnp.int32, (t_keys, LANES), 0).astype(F32)
        hi, mid, lo = _split3(a * kpos)
        feat = jnp.where(klane == 3, hi, jnp.where(klane == 4, mid, jnp.where(
            klane == 5, lo, 0.0)))
        kp_ref[:, LANES:] = jnp.where(klane < 3, 1.0, feat).astype(BF16)
        rel = (lax.broadcasted_iota(jnp.int32, (tq, tk), 0)
               - lax.broadcasted_iota(jnp.int32, (tq, tk), 1)).astype(F32)
        for par in range(ratio):
            diag_ref[par] = -a * jnp.abs(rel + float(par * tq))

    block_info = []
    for blk in range(DF_BLOCKS):
        ib = i * DF_BLOCKS + blk
        q = q_ref[blk * tq:(blk + 1) * tq, :]
        zero = jnp.zeros_like(q)
        qpos = (ib * tq + lax.broadcasted_iota(jnp.int32, (tq, LANES), 0)).astype(F32)
        hi, mid, lo = _split3(-a * qpos)
        before = jnp.where(lane == 0, hi, jnp.where(lane == 1, mid, jnp.where(
            lane == 2, lo, jnp.where(lane < 6, 1.0, 0.0)))).astype(BF16)
        for half in range(2):
            qz = jnp.where(lane < DF_DH, q, zero) if half == 0 else jnp.where(
                lane < DF_DH, zero, q)
            for kind, feat in enumerate((before, -before, zero)):
                qv_ref[blk, half, kind, :, :LANES] = qz
                qv_ref[blk, half, kind, :, LANES:] = feat
        block_info.append((lax.div(ib, ratio), lax.rem(ib, ratio)))

    def chunk_rows(blk, d):
        j_diag = block_info[blk][0]
        j = lax.rem(j_diag + d, n_chunks)
        return j, pl.ds(pl.multiple_of(j * tk, tk), tk)

    def lane_groups(x):
        return [x[:, g * LANES:(g + 1) * LANES] for g in range(groups)]

    def score_chunk(n, d, mx):
        blk, half = divmod(n, 2)
        j_diag, par = block_info[blk]
        j, rows = chunk_rows(blk, d)
        dims = (((1,), (1,)), ((), ()))
        if d == 0:
            t = lax.dot_general(qv_ref[blk, half, 2], kp_ref[rows, :], dims,
                                preferred_element_type=F32) + diag_ref[par]
        else:
            kind = jnp.where(j < j_diag, 0, 1)
            t = lax.dot_general(qv_ref[blk, half, kind], kp_ref[rows, :], dims,
                                preferred_element_type=F32)
        t_ref[n % 2, :, d * tk:(d + 1) * tk] = t
        tg = lane_groups(t)
        while len(tg) > 1:
            tg = [jnp.maximum(x, y) for x, y in zip(tg[::2], tg[1::2])]
        return jnp.maximum(mx, tg[0])

    def value_chunk(n, d, m, lsum, acc):
        _, rows = chunk_rows(n // 2, d)
        pg = [jnp.exp2(x - m) for x in lane_groups(t_ref[n % 2, :, d * tk:(d + 1) * tk])]
        for x in pg:
            lsum = lsum + x
        p = jnp.concatenate([x.astype(BF16) for x in pg], axis=1)
        return lsum, acc + jnp.dot(p, v_ref[rows, :], preferred_element_type=F32)

    n_sets = 2 * DF_BLOCKS
    neg = jnp.full((tq, LANES), NEG_BIG, F32)
    zeros = jnp.zeros((tq, LANES), F32)
    mx = neg
    for d in range(n_chunks):
        mx = score_chunk(0, d, mx)
    outs = []
    for n in range(n_sets):
        m = jnp.broadcast_to(jnp.max(mx, axis=-1, keepdims=True), (tq, LANES))
        mx, lsum, acc = neg, zeros, zeros
        for d in range(n_chunks):
            if n + 1 < n_sets:
                mx = score_chunk(n + 1, d, mx)
            lsum, acc = value_chunk(n, d, m, lsum, acc)
        outs.append(acc / jnp.sum(lsum, axis=-1, keepdims=True))

    lam = (jnp.exp(jnp.sum(lq1_ref[...] * lk1_ref[...], axis=-1, keepdims=True))
           - jnp.exp(jnp.sum(lq2_ref[...] * lk2_ref[...], axis=-1, keepdims=True))
           + LAM_INIT)
    for blk in range(DF_BLOCKS):
        o = outs[2 * blk] - lam * outs[2 * blk + 1]
        y = o * lax.rsqrt(jnp.mean(o * o, axis=-1, keepdims=True) + LN_EPS)
        o_ref[blk * tq:(blk + 1) * tq, :] = (y * g_ref[...] * (1.0 - LAM_INIT)).astype(
            o_ref.dtype)


def _na_table_kernel(rpb_ref, o_ref):
    pair = pl.program_id(0)
    e = pl.program_id(1)
    n_dr, n_dc = 2 * NA_ROWS - 1, 2 * NA_COLS - 1
    c = lax.broadcasted_iota(jnp.int32, (GRID_W, LANES), 0)
    lane = lax.broadcasted_iota(jnp.int32, (GRID_W, LANES), 1)
    cc = lane & (GRID_W - 1)
    c0 = jnp.clip(c - NA_COLS // 2, 0, GRID_W - NA_COLS)
    in_window = (cc >= c0) & (cc < c0 + NA_COLS)
    dc = jnp.where(in_window, cc - c + (NA_COLS - 1), -1)
    upper = lane >= GRID_W
    for half in range(2):
        head = 2 * pair + half
        for jj in range(NA_ROWS // 2):
            base = (head * n_dr + (2 * jj - e + NA_ROWS - 1)) * n_dc
            acc = jnp.full((GRID_W, LANES), NEG_BIG, F32)
            for x in range(n_dc):
                val = jnp.where(upper, rpb_ref[base + n_dc + x], rpb_ref[base + x])
                acc = jnp.where(dc == x, val, acc)
            o_ref[0, 0, half * GRID_W:(half + 1) * GRID_W, jj * LANES:(jj + 1) * LANES] = (
                acc * LOG2E)


def _na_bias_table(rpb):
    tile = (2 * GRID_W, NA_ROWS * GRID_W)
    return pl.pallas_call(
        _na_table_kernel,
        grid=(NA_HEADS // 2, NA_ROWS),
        in_specs=[pl.BlockSpec(memory_space=pltpu.SMEM)],
        out_specs=pl.BlockSpec((1, 1, *tile), lambda a, e: (a, e, 0, 0)),
        out_shape=jax.ShapeDtypeStruct((NA_HEADS // 2, NA_ROWS, *tile), F32),
        compiler_params=_params(2),
        name="na_bias_table",
    )(rpb.reshape(-1))


def _resident(shape):
    return pl.BlockSpec(shape, lambda *_: (0,) * len(shape), pipeline_mode=pl.Buffered(1))


def _params(n_axes):
    return pltpu.CompilerParams(dimension_semantics=("arbitrary",) * n_axes,
                                vmem_limit_bytes=VMEM_LIMIT)


def _ffn_weight_scratch():
    return [pltpu.VMEM((D_MODEL, D_FF), BF16), pltpu.VMEM((D_MODEL, D_FF), BF16),
            pltpu.VMEM((D_FF, D_MODEL), BF16)]


def _stage(cols):
    return pltpu.VMEM((2, CAST_ROWS, cols), F32)


HBM_SPEC = pl.BlockSpec(memory_space=pl.ANY)


def _ffn1_call(x2d, wg, wu, wd, g, b, w_in):
    n = x2d.shape[0]
    row_spec = pl.BlockSpec((FFN_ROWS, D_MODEL), lambda i: (i, 0))
    vec_spec = _resident((1, D_MODEL))
    return pl.pallas_call(
        _ffn1_kernel,
        grid=(n // FFN_ROWS,),
        in_specs=[row_spec, HBM_SPEC, HBM_SPEC, HBM_SPEC, vec_spec, vec_spec, HBM_SPEC],
        out_specs=[row_spec, pl.BlockSpec((FFN_ROWS, IN_COLS), lambda i: (i, 0))],
        out_shape=[jax.ShapeDtypeStruct((n, D_MODEL), F32),
                   jax.ShapeDtypeStruct((n, IN_COLS), BF16)],
        scratch_shapes=[pltpu.VMEM((FFN_ROWS, D_MODEL), BF16),
                        pltpu.VMEM((FFN_ROWS, D_MODEL), F32),
                        *_ffn_weight_scratch(), pltpu.VMEM((D_MODEL, IN_COLS), BF16),
                        _stage(D_FF), _stage(D_MODEL), _stage(IN_COLS),
                        pltpu.SemaphoreType.DMA((2,))],
        compiler_params=_params(1),
        name="ffn1_ln1_qkv",
    )(x2d, wg, wu, wd, g, b, w_in)


def _ffn2_call(x1, na_out, df_out, w_out, g2, b2, wg, wu, wd, g3, b3):
    n = x1.shape[0]
    row_spec = pl.BlockSpec((FFN_ROWS, D_MODEL), lambda i: (i, 0))
    half_spec = pl.BlockSpec((FFN_ROWS, NA_WIDTH), lambda i: (i, 0))
    vec_spec = _resident((1, D_MODEL))
    return pl.pallas_call(
        _ffn2_kernel,
        grid=(n // FFN_ROWS,),
        in_specs=[row_spec, half_spec, half_spec, HBM_SPEC,
                  vec_spec, vec_spec, HBM_SPEC, HBM_SPEC, HBM_SPEC, vec_spec, vec_spec],
        out_specs=row_spec,
        out_shape=jax.ShapeDtypeStruct((n, D_MODEL), F32),
        scratch_shapes=[pltpu.VMEM((FFN_ROWS, D_MODEL), F32),
                        pltpu.VMEM((FFN_ROWS, D_MODEL), BF16),
                        pltpu.VMEM((FFN_ROWS, D_MODEL), F32),
                        *_ffn_weight_scratch(), pltpu.VMEM((D_MODEL, D_MODEL), BF16),
                        _stage(D_FF), _stage(D_MODEL), pltpu.SemaphoreType.DMA((2,))],
        compiler_params=_params(1),
        name="outproj_ln2_ffn2_ln3",
    )(x1, na_out, df_out, w_out, g2, b2, wg, wu, wd, g3, b3)


def _na_call(p, tbl, batch, seq):
    step_rows = NA_ROWS_PER_STEP * GRID_W
    steps = seq // step_rows
    q_spec = pl.BlockSpec((step_rows, NA_WIDTH), lambda b, i: (b * steps + i, 0))
    return pl.pallas_call(
        _na_kernel,
        grid=(batch, steps),
        in_specs=[q_spec,
                  pl.BlockSpec((seq, NA_WIDTH), lambda b, i: (b, 1)),
                  pl.BlockSpec((seq, NA_WIDTH), lambda b, i: (b, 2)),
                  _resident(tbl.shape)],
        out_specs=q_spec,
        out_shape=jax.ShapeDtypeStruct((batch * seq, NA_WIDTH), BF16),
        compiler_params=_params(2),
        name="neighbourhood_attention",
    )(p, p, p, tbl)


def _diff_call(p, slopes, lq1, lk1, lq2, lk2, subln_g, batch, seq):
    step_rows = DF_BLOCKS * DF_TQ
    steps = seq // step_rows
    q_col0 = 3 * NA_WIDTH // LANES
    k_col0 = q_col0 + DF_WIDTH // LANES
    v_col0 = k_col0 + DF_WIDTH // LANES
    lam_spec = _resident((1, DF_DH))
    return pl.pallas_call(
        _diff_kernel,
        grid=(batch, DF_HEADS, steps),
        in_specs=[pl.BlockSpec(memory_space=pltpu.SMEM),
                  pl.BlockSpec((step_rows, LANES), lambda b, h, i: (b * steps + i, q_col0 + h)),
                  pl.BlockSpec((seq, LANES), lambda b, h, i: (b, k_col0 + h)),
                  pl.BlockSpec((seq, LANES), lambda b, h, i: (b, v_col0 + h)),
                  lam_spec, lam_spec, lam_spec, lam_spec,
                  _resident((1, 2 * DF_DH))],
        out_specs=pl.BlockSpec((step_rows, LANES), lambda b, h, i: (b * steps + i, h)),
        out_shape=jax.ShapeDtypeStruct((batch * seq, DF_WIDTH), BF16),
        scratch_shapes=[pltpu.VMEM((seq, 2 * LANES), BF16),
                        pltpu.VMEM((DF_BLOCKS, 2, 3, DF_TQ, 2 * LANES), BF16),
                        pltpu.VMEM((DF_TK // DF_TQ, DF_TQ, DF_TK), F32),
                        pltpu.VMEM((2, DF_TQ, seq), F32)],
        compiler_params=_params(3),
        name="differential_attention",
    )(slopes, p, p, p, lq1, lk1, lq2, lk2, subln_g)


def kernel(x, ln1_g, ln1_b, ffn1_w_gate, ffn1_w_up, ffn1_w_down, w_in, na_rpb,
           diff_lambda_q1, diff_lambda_k1, diff_lambda_q2, diff_lambda_k2, diff_subln_g,
           w_out, ln2_g, ln2_b, ffn2_w_gate, ffn2_w_up, ffn2_w_down, ln3_g, ln3_b):
    batch, seq, d = x.shape
    assert d == D_MODEL and seq % (NA_ROWS_PER_STEP * GRID_W) == 0
    assert seq % DF_TK == 0 and DF_TK % DF_TQ == 0 and seq % (DF_BLOCKS * DF_TQ) == 0
    assert ln1_g.shape[0] == DEPTH == 1
    x2d = x.reshape(batch * seq, d)
    x1, p = _ffn1_call(x2d, ffn1_w_gate[0], ffn1_w_up[0], ffn1_w_down[0], ln1_g, ln1_b,
                       w_in[0])
    na_out = _na_call(p, _na_bias_table(na_rpb[0]), batch, seq)
    slopes = jnp.exp2(-8.0 * jnp.arange(1, DF_HEADS + 1, dtype=F32) / DF_HEADS)
    df_out = _diff_call(p, slopes, diff_lambda_q1, diff_lambda_k1, diff_lambda_q2,
                        diff_lambda_k2, diff_subln_g, batch, seq)
    out = _ffn2_call(x1, na_out, df_out, w_out[0], ln2_g, ln2_b,
                     ffn2_w_gate[0], ffn2_w_up[0], ffn2_w_down[0], ln3_g, ln3_b)
    return out.reshape(batch, seq, d)
```

```python
import functools
import math

import jax
import jax.numpy as jnp
from jax import lax
from jax.experimental import pallas as pl
from jax.experimental.pallas import tpu as pltpu

F32 = jnp.float32
BF16 = jnp.bfloat16

D_MODEL = 1024
D_FF = 2816
DEPTH = 1
GRID_W = 64
NA_HEADS, NA_DH = 8, 64
NA_ROWS, NA_COLS = 8, 16
DF_HEADS, DF_DH = 4, 64
NA_WIDTH = NA_HEADS * NA_DH
DF_WIDTH = DF_HEADS * 2 * DF_DH
IN_COLS = 3 * NA_WIDTH + 3 * DF_WIDTH
ALPHA = (2.0 * DEPTH) ** 0.25
LN_EPS = 1e-5
LAM_INIT = 0.8 - 0.6 * math.exp(-0.3 * 0)
NEG_BIG = -1e30
LOG2E = math.log2(math.e)

LANES = 128
MXU_DIM = 256
FF_CHUNK = MXU_DIM
N_FF_CHUNKS = D_FF // FF_CHUNK
FFN_ROWS = 512
FFN_SUB = 256
FFN_SKEW = 2
NA_ROWS_PER_STEP = 8
DF_TQ = 256
DF_TK = 512
DF_BLOCKS = 4
VMEM_LIMIT = 56 * 1024 * 1024


def _layer_norm(y, g, b):
    mu = jnp.mean(y, axis=-1, keepdims=True)
    yc = y - mu
    var = jnp.mean(yc * yc, axis=-1, keepdims=True)
    return yc * lax.rsqrt(var + LN_EPS) * g + b


def _swiglu_lanes(xb_ref, wg_ref, wu_ref, wd_ref, acc_ref, before_lane=None, after_lane=None):
    lanes = [slice(r, r + FFN_SUB) for r in range(0, FFN_ROWS, FFN_SUB)]
    pending = [None] * len(lanes)
    if before_lane is not None:
        for rows in lanes:
            before_lane(rows)

    def slot(k, s):
        rows = lanes[k]
        h_new = None
        if s < N_FF_CHUNKS:
            xb = xb_ref[rows, :]
            cols = slice(s * FF_CHUNK, (s + 1) * FF_CHUNK)
            g = jnp.dot(xb, wg_ref[:, cols], preferred_element_type=F32)
            u = jnp.dot(xb, wu_ref[:, cols], preferred_element_type=F32)
            h_new = (g * jax.nn.sigmoid(g) * u).astype(BF16)
        if s >= 1:
            d = jnp.dot(pending[k], wd_ref[(s - 1) * FF_CHUNK:s * FF_CHUNK, :],
                        preferred_element_type=F32)
            if s == 1:
                acc_ref[rows, :] = d
            else:
                acc_ref[rows, :] += d
        pending[k] = h_new

    n_slots = N_FF_CHUNKS + 1
    finished = []
    for t in range(n_slots + FFN_SKEW * (len(lanes) - 1)):
        for k in range(len(lanes)):
            s = t - FFN_SKEW * k
            if 0 <= s < n_slots:
                slot(k, s)
                if s == n_slots - 1:
                    finished.append(k)
    if after_lane is not None:
        for k in finished:
            after_lane(lanes[k])


def _ffn1_kernel(x_ref, wg_ref, wu_ref, wd_ref, g_ref, b_ref, win_ref,
                 x1_ref, p_ref, xb_ref, acc_ref):
    xb_ref[...] = x_ref[...].astype(BF16)

    def project(rows):
        x1 = _layer_norm(ALPHA * x_ref[rows, :] + 0.5 * acc_ref[rows, :],
                         g_ref[...], b_ref[...])
        x1_ref[rows, :] = x1
        x1b = x1.astype(BF16)
        group_scale = (NA_DH ** -0.5 * LOG2E, None, None, DF_DH ** -0.5 * LOG2E, None, None)
        for j, gs in enumerate(group_scale):
            cols = slice(j * NA_WIDTH, (j + 1) * NA_WIDTH)
            pj = jnp.dot(x1b, win_ref[:, cols], preferred_element_type=F32)
            if gs is not None:
                pj = pj * gs
            p_ref[rows, cols] = pj.astype(BF16)

    _swiglu_lanes(xb_ref, wg_ref, wu_ref, wd_ref, acc_ref, after_lane=project)


def _ffn2_kernel(x1_ref, na_ref, df_ref, wout_ref, g2_ref, b2_ref,
                 wg_ref, wu_ref, wd_ref, g3_ref, b3_ref,
                 o_ref, x2_ref, xb_ref, acc_ref):
    def mix_in(rows):
        mix = jnp.dot(na_ref[rows, :], wout_ref[:NA_WIDTH, :], preferred_element_type=F32)
        mix += jnp.dot(df_ref[rows, :], wout_ref[NA_WIDTH:, :], preferred_element_type=F32)
        x2 = _layer_norm(ALPHA * x1_ref[rows, :] + mix, g2_ref[...], b2_ref[...])
        x2_ref[rows, :] = x2
        xb_ref[rows, :] = x2.astype(BF16)

    def finish(rows):
        o_ref[rows, :] = _layer_norm(ALPHA * x2_ref[rows, :] + 0.5 * acc_ref[rows, :],
                                     g3_ref[...], b3_ref[...])

    _swiglu_lanes(xb_ref, wg_ref, wu_ref, wd_ref, acc_ref, before_lane=mix_in,
                  after_lane=finish)


def _na_kernel(q_ref, k_ref, v_ref, tbl_ref, o_ref):
    rows_total = k_ref.shape[0] // GRID_W
    win_keys = NA_ROWS * GRID_W
    lane = lax.broadcasted_iota(jnp.int32, (GRID_W, LANES), 1)
    lo = lane < NA_DH

    def window(rr):
        r = pl.program_id(1) * NA_ROWS_PER_STEP + rr
        r0 = jnp.clip(r - NA_ROWS // 2, 0, rows_total - NA_ROWS)
        return pl.ds(pl.multiple_of(r0 * GRID_W, GRID_W), win_keys), r - r0

    def scores(rr):
        krows, e = window(rr)
        out = []
        for hp in range(NA_HEADS // 2):
            cols = slice(hp * LANES, (hp + 1) * LANES)
            q = q_ref[rr * GRID_W:(rr + 1) * GRID_W, cols]
            zero = jnp.zeros_like(q)
            qz = jnp.concatenate([jnp.where(lo, q, zero), jnp.where(lo, zero, q)], axis=0)
            s = lax.dot_general(qz, k_ref[krows, cols], (((1,), (1,)), ((), ())),
                                preferred_element_type=F32)
            bias = jnp.concatenate([jnp.concatenate(
                [tbl_ref[2 * hp + par, 2 * w - e + NA_ROWS - 1] for w in range(NA_ROWS // 2)],
                axis=1) for par in range(2)], axis=0)
            out.append(s + bias)
        return out

    def finish(rr, s_list):
        krows, _ = window(rr)
        for hp, s in enumerate(s_list):
            cols = slice(hp * LANES, (hp + 1) * LANES)
            m = jnp.max(s, axis=-1, keepdims=True)
            p = jnp.exp2(s - m)
            l = jnp.sum(p, axis=-1, keepdims=True)
            o = jnp.dot(p.astype(BF16), v_ref[krows, cols], preferred_element_type=F32)
            o = o / l
            o_ref[rr * GRID_W:(rr + 1) * GRID_W, cols] = jnp.where(
                lo, o[:GRID_W], o[GRID_W:]).astype(o_ref.dtype)

    pending = scores(0)
    for rr in range(NA_ROWS_PER_STEP):
        upcoming = scores(rr + 1) if rr + 1 < NA_ROWS_PER_STEP else None
        finish(rr, pending)
        pending = upcoming


def _split3(x):
    hi = x.astype(BF16).astype(F32)
    mid = (x - hi).astype(BF16).astype(F32)
    return hi, mid, (x - hi - mid).astype(BF16).astype(F32)


def _diff_kernel(slope_ref, q_ref, k_ref, v_ref, lq1_ref, lk1_ref, lq2_ref, lk2_ref, g_ref,
                 o_ref, kp_ref, qv_ref, diag_ref, t_ref):
    tq, tk = DF_TQ, DF_TK
    n_chunks = k_ref.shape[0] // tk
    ratio = tk // tq
    groups = tk // LANES
    h = pl.program_id(1)
    i = pl.program_id(2)
    a = slope_ref[h] * LOG2E
    lane = lax.broadcasted_iota(jnp.int32, (tq, LANES), 1)

    @pl.when(i == 0)
    def _per_head_setup():
        t_keys = k_ref.shape[0]
        kp_ref[:, :LANES] = k_ref[...]
        klane = lax.broadcasted_iota(jnp.int32, (t_keys, LANES), 1)
        kpos = lax.broadcasted_iota(jnp.int32, (t_keys, LANES), 0).astype(F32)
        hi, mid, lo = _split3(a * kpos)
        feat = jnp.where(klane == 3, hi, jnp.where(klane == 4, mid, jnp.where(
            klane == 5, lo, 0.0)))
        kp_ref[:, LANES:] = jnp.where(klane < 3, 1.0, feat).astype(BF16)
        rel = (lax.broadcasted_iota(jnp.int32, (tq, tk), 0)
               - lax.broadcasted_iota(jnp.int32, (tq, tk), 1)).astype(F32)
        for par in range(ratio):
            diag_ref[par] = -a * jnp.abs(rel + float(par * tq))

    block_info = []
    for blk in range(DF_BLOCKS):
        ib = i * DF_BLOCKS + blk
        q = q_ref[blk * tq:(blk + 1) * tq, :]
        zero = jnp.zeros_like(q)
        qpos = (ib * tq + lax.broadcasted_iota(jnp.int32, (tq, LANES), 0)).astype(F32)
        hi, mid, lo = _split3(-a * qpos)
        before = jnp.where(lane == 0, hi, jnp.where(lane == 1, mid, jnp.where(
            lane == 2, lo, jnp.where(lane < 6, 1.0, 0.0)))).astype(BF16)
        for half in range(2):
            qz = jnp.where(lane < DF_DH, q, zero) if half == 0 else jnp.where(
                lane < DF_DH, zero, q)
            for kind, feat in enumerate((before, -before, zero)):
                qv_ref[blk, half, kind, :, :LANES] = qz
                qv_ref[blk, half, kind, :, LANES:] = feat
        block_info.append((lax.div(ib, ratio), lax.rem(ib, ratio)))

    def chunk_rows(blk, d):
        j_diag = block_info[blk][0]
        j = lax.rem(j_diag + d, n_chunks)
        return j, pl.ds(pl.multiple_of(j * tk, tk), tk)

    def lane_groups(x):
        return [x[:, g * LANES:(g + 1) * LANES] for g in range(groups)]

    def score_chunk(n, d, mx):
        blk, half = divmod(n, 2)
        j_diag, par = block_info[blk]
        j, rows = chunk_rows(blk, d)
        dims = (((1,), (1,)), ((), ()))
        if d == 0:
            t = lax.dot_general(qv_ref[blk, half, 2], kp_ref[rows, :], dims,
                                preferred_element_type=F32) + diag_ref[par]
        else:
            kind = jnp.where(j < j_diag, 0, 1)
            t = lax.dot_general(qv_ref[blk, half, kind], kp_ref[rows, :], dims,
                                preferred_element_type=F32)
        t_ref[n % 2, :, d * tk:(d + 1) * tk] = t
        tg = lane_groups(t)
        while len(tg) > 1:
            tg = [jnp.maximum(x, y) for x, y in zip(tg[::2], tg[1::2])]
        return jnp.maximum(mx, tg[0])

    def value_chunk(n, d, m, lsum, acc):
        _, rows = chunk_rows(n // 2, d)
        pg = [jnp.exp2(x - m) for x in lane_groups(t_ref[n % 2, :, d * tk:(d + 1) * tk])]
        for x in pg:
            lsum = lsum + x
        p = jnp.concatenate([x.astype(BF16) for x in pg], axis=1)
        return lsum, acc + jnp.dot(p, v_ref[rows, :], preferred_element_type=F32)

    n_sets = 2 * DF_BLOCKS
    neg = jnp.full((tq, LANES), NEG_BIG, F32)
    zeros = jnp.zeros((tq, LANES), F32)
    mx = neg
    for d in range(n_chunks):
        mx = score_chunk(0, d, mx)
    outs = []
    for n in range(n_sets):
        m = jnp.broadcast_to(jnp.max(mx, axis=-1, keepdims=True), (tq, LANES))
        mx, lsum, acc = neg, zeros, zeros
        for d in range(n_chunks):
            if n + 1 < n_sets:
                mx = score_chunk(n + 1, d, mx)
            lsum, acc = value_chunk(n, d, m, lsum, acc)
        outs.append(acc / jnp.sum(lsum, axis=-1, keepdims=True))

    lam = (jnp.exp(jnp.sum(lq1_ref[...] * lk1_ref[...], axis=-1, keepdims=True))
           - jnp.exp(jnp.sum(lq2_ref[...] * lk2_ref[...], axis=-1, keepdims=True))
           + LAM_INIT)
    for blk in range(DF_BLOCKS):
        o = outs[2 * blk] - lam * outs[2 * blk + 1]
        y = o * lax.rsqrt(jnp.mean(o * o, axis=-1, keepdims=True) + LN_EPS)
        o_ref[blk * tq:(blk + 1) * tq, :] = (y * g_ref[...] * (1.0 - LAM_INIT)).astype(
            o_ref.dtype)


def _na_table_kernel(rpb_ref, o_ref):
    head = pl.program_id(0)
    n_dr, n_dc = 2 * NA_ROWS - 1, 2 * NA_COLS - 1
    c = lax.broadcasted_iota(jnp.int32, (GRID_W, LANES), 0)
    lane = lax.broadcasted_iota(jnp.int32, (GRID_W, LANES), 1)
    cc = lane & (GRID_W - 1)
    c0 = jnp.clip(c - NA_COLS // 2, 0, GRID_W - NA_COLS)
    in_window = (cc >= c0) & (cc < c0 + NA_COLS)
    dc = jnp.where(in_window, cc - c + (NA_COLS - 1), -1)
    upper = lane >= GRID_W
    for d in range(n_dr - 1):
        base = (head * n_dr + d) * n_dc
        acc = jnp.full((GRID_W, LANES), NEG_BIG, F32)
        for x in range(n_dc):
            val = jnp.where(upper, rpb_ref[base + n_dc + x], rpb_ref[base + x])
            acc = jnp.where(dc == x, val, acc)
        o_ref[0, d] = acc * LOG2E


def _na_bias_table(rpb):
    shape = (NA_HEADS, 2 * NA_ROWS - 2, GRID_W, LANES)
    return pl.pallas_call(
        _na_table_kernel,
        grid=(NA_HEADS,),
        in_specs=[pl.BlockSpec(memory_space=pltpu.SMEM)],
        out_specs=pl.BlockSpec((1, *shape[1:]), lambda h: (h, 0, 0, 0)),
        out_shape=jax.ShapeDtypeStruct(shape, F32),
        compiler_params=_params(1),
        name="na_bias_table",
    )(rpb.reshape(-1))


def _resident(shape):
    return pl.BlockSpec(shape, lambda *_: (0,) * len(shape), pipeline_mode=pl.Buffered(1))


def _params(n_axes):
    return pltpu.CompilerParams(dimension_semantics=("arbitrary",) * n_axes,
                                vmem_limit_bytes=VMEM_LIMIT)


def _ffn_weight_specs():
    return [_resident((D_MODEL, D_FF)), _resident((D_MODEL, D_FF)), _resident((D_FF, D_MODEL))]


def _ffn1_call(x2d, wg, wu, wd, g, b, w_in):
    n = x2d.shape[0]
    row_spec = pl.BlockSpec((FFN_ROWS, D_MODEL), lambda i: (i, 0))
    vec_spec = _resident((1, D_MODEL))
    return pl.pallas_call(
        _ffn1_kernel,
        grid=(n // FFN_ROWS,),
        in_specs=[row_spec, *_ffn_weight_specs(), vec_spec, vec_spec,
                  _resident((D_MODEL, IN_COLS))],
        out_specs=[row_spec, pl.BlockSpec((FFN_ROWS, IN_COLS), lambda i: (i, 0))],
        out_shape=[jax.ShapeDtypeStruct((n, D_MODEL), F32),
                   jax.ShapeDtypeStruct((n, IN_COLS), BF16)],
        scratch_shapes=[pltpu.VMEM((FFN_ROWS, D_MODEL), BF16),
                        pltpu.VMEM((FFN_ROWS, D_MODEL), F32)],
        compiler_params=_params(1),
        name="ffn1_ln1_qkv",
    )(x2d, wg, wu, wd, g, b, w_in)


def _ffn2_call(x1, na_out, df_out, w_out, g2, b2, wg, wu, wd, g3, b3):
    n = x1.shape[0]
    row_spec = pl.BlockSpec((FFN_ROWS, D_MODEL), lambda i: (i, 0))
    half_spec = pl.BlockSpec((FFN_ROWS, NA_WIDTH), lambda i: (i, 0))
    vec_spec = _resident((1, D_MODEL))
    return pl.pallas_call(
        _ffn2_kernel,
        grid=(n // FFN_ROWS,),
        in_specs=[row_spec, half_spec, half_spec, _resident((D_MODEL, D_MODEL)),
                  vec_spec, vec_spec, *_ffn_weight_specs(), vec_spec, vec_spec],
        out_specs=row_spec,
        out_shape=jax.ShapeDtypeStruct((n, D_MODEL), F32),
        scratch_shapes=[pltpu.VMEM((FFN_ROWS, D_MODEL), F32),
                        pltpu.VMEM((FFN_ROWS, D_MODEL), BF16),
                        pltpu.VMEM((FFN_ROWS, D_MODEL), F32)],
        compiler_params=_params(1),
        name="outproj_ln2_ffn2_ln3",
    )(x1, na_out, df_out, w_out, g2, b2, wg, wu, wd, g3, b3)


def _na_call(p, tbl, batch, seq):
    step_rows = NA_ROWS_PER_STEP * GRID_W
    steps = seq // step_rows
    q_spec = pl.BlockSpec((step_rows, NA_WIDTH), lambda b, i: (b * steps + i, 0))
    return pl.pallas_call(
        _na_kernel,
        grid=(batch, steps),
        in_specs=[q_spec,
                  pl.BlockSpec((seq, NA_WIDTH), lambda b, i: (b, 1)),
                  pl.BlockSpec((seq, NA_WIDTH), lambda b, i: (b, 2)),
                  _resident(tbl.shape)],
        out_specs=q_spec,
        out_shape=jax.ShapeDtypeStruct((batch * seq, NA_WIDTH), BF16),
        compiler_params=_params(2),
        name="neighbourhood_attention",
    )(p, p, p, tbl)


def _diff_call(p, slopes, lq1, lk1, lq2, lk2, subln_g, batch, seq):
    step_rows = DF_BLOCKS * DF_TQ
    steps = seq // step_rows
    q_col0 = 3 * NA_WIDTH // LANES
    k_col0 = q_col0 + DF_WIDTH // LANES
    v_col0 = k_col0 + DF_WIDTH // LANES
    lam_spec = _resident((1, DF_DH))
    return pl.pallas_call(
        _diff_kernel,
        grid=(batch, DF_HEADS, steps),
        in_specs=[pl.BlockSpec(memory_space=pltpu.SMEM),
                  pl.BlockSpec((step_rows, LANES), lambda b, h, i: (b * steps + i, q_col0 + h)),
                  pl.BlockSpec((seq, LANES), lambda b, h, i: (b, k_col0 + h)),
                  pl.BlockSpec((seq, LANES), lambda b, h, i: (b, v_col0 + h)),
                  lam_spec, lam_spec, lam_spec, lam_spec,
                  _resident((1, 2 * DF_DH))],
        out_specs=pl.BlockSpec((step_rows, LANES), lambda b, h, i: (b * steps + i, h)),
        out_shape=jax.ShapeDtypeStruct((batch * seq, DF_WIDTH), BF16),
        scratch_shapes=[pltpu.VMEM((seq, 2 * LANES), BF16),
                        pltpu.VMEM((DF_BLOCKS, 2, 3, DF_TQ, 2 * LANES), BF16),
                        pltpu.VMEM((DF_TK // DF_TQ, DF_TQ, DF_TK), F32),
                        pltpu.VMEM((2, DF_TQ, seq), F32)],
        compiler_params=_params(3),
        name="differential_attention",
    )(slopes, p, p, p, lq1, lk1, lq2, lk2, subln_g)


def kernel(x, ln1_g, ln1_b, ffn1_w_gate, ffn1_w_up, ffn1_w_down, w_in, na_rpb,
           diff_lambda_q1, diff_lambda_k1, diff_lambda_q2, diff_lambda_k2, diff_subln_g,
           w_out, ln2_g, ln2_b, ffn2_w_gate, ffn2_w_up, ffn2_w_down, ln3_g, ln3_b):
    batch, seq, d = x.shape
    assert d == D_MODEL and seq % (NA_ROWS_PER_STEP * GRID_W) == 0
    assert seq % DF_TK == 0 and DF_TK % DF_TQ == 0 and seq % (DF_BLOCKS * DF_TQ) == 0
    assert ln1_g.shape[0] == DEPTH == 1
    x2d = x.reshape(batch * seq, d)
    x1, p = _ffn1_call(x2d, ffn1_w_gate[0].astype(BF16), ffn1_w_up[0].astype(BF16),
                       ffn1_w_down[0].astype(BF16), ln1_g, ln1_b, w_in[0].astype(BF16))
    na_out = _na_call(p, _na_bias_table(na_rpb[0]), batch, seq)
    slopes = jnp.exp2(-8.0 * jnp.arange(1, DF_HEADS + 1, dtype=F32) / DF_HEADS)
    df_out = _diff_call(p, slopes, diff_lambda_q1, diff_lambda_k1, diff_lambda_q2,
                        diff_lambda_k2, diff_subln_g, batch, seq)
    out = _ffn2_call(x1, na_out, df_out, w_out[0].astype(BF16), ln2_g, ln2_b,
                     ffn2_w_gate[0].astype(BF16), ffn2_w_up[0].astype(BF16),
                     ffn2_w_down[0].astype(BF16), ln3_g, ln3_b)
    return out.reshape(batch, seq, d)
```

```python
import functools
import math

import jax
import jax.numpy as jnp
from jax import lax
from jax.experimental import pallas as pl
from jax.experimental.pallas import tpu as pltpu

F32 = jnp.float32
BF16 = jnp.bfloat16

D_MODEL = 1024
D_FF = 2816
DEPTH = 1
GRID_W = 64
NA_HEADS, NA_DH = 8, 64
NA_ROWS, NA_COLS = 8, 16
DF_HEADS, DF_DH = 4, 64
NA_WIDTH = NA_HEADS * NA_DH
DF_WIDTH = DF_HEADS * 2 * DF_DH
IN_COLS = 3 * NA_WIDTH + 3 * DF_WIDTH
P_COLS = IN_COLS - DF_WIDTH
ALPHA = (2.0 * DEPTH) ** 0.25
LN_EPS = 1e-5
LAM_INIT = 0.8 - 0.6 * math.exp(-0.3 * 0)
NEG_BIG = -1e30
LOG2E = math.log2(math.e)

LANES = 128
SUBLANES = 8
MXU_DIM = 256
FF_CHUNK = MXU_DIM
N_FF_CHUNKS = D_FF // FF_CHUNK
FFN_ROWS = 512
FFN_SUB = 256
FFN_SKEW = 2
NA_ROWS_PER_STEP = 8
DF_TQ = 256
DF_TK = 512
DF_BLOCKS = 8
DF_STAT_ACCS = 2
DF_STREAMS = 2
DF_RING = 2 * DF_STREAMS
VMEM_LIMIT = 56 * 1024 * 1024


def _layer_norm(y, g, b):
    mu = jnp.mean(y, axis=-1, keepdims=True)
    yc = y - mu
    var = jnp.mean(yc * yc, axis=-1, keepdims=True)
    return yc * lax.rsqrt(var + LN_EPS) * g + b


def _swiglu_lanes(xb_ref, wg_ref, wu_ref, wd_ref, acc_ref, before_lane=None, after_lane=None):
    lanes = [slice(r, r + FFN_SUB) for r in range(0, FFN_ROWS, FFN_SUB)]
    pending = [None] * len(lanes)
    if before_lane is not None:
        for rows in lanes:
            before_lane(rows)

    def slot(k, s):
        rows = lanes[k]
        h_new = None
        if s < N_FF_CHUNKS:
            xb = xb_ref[rows, :]
            cols = slice(s * FF_CHUNK, (s + 1) * FF_CHUNK)
            g = jnp.dot(xb, wg_ref[:, cols], preferred_element_type=F32)
            u = jnp.dot(xb, wu_ref[:, cols], preferred_element_type=F32)
            h_new = (g * jax.nn.sigmoid(g) * u).astype(BF16)
        if s >= 1:
            d = jnp.dot(pending[k], wd_ref[(s - 1) * FF_CHUNK:s * FF_CHUNK, :],
                        preferred_element_type=F32)
            if s == 1:
                acc_ref[rows, :] = d
            else:
                acc_ref[rows, :] += d
        pending[k] = h_new

    n_slots = N_FF_CHUNKS + 1
    finished = []
    for t in range(n_slots + FFN_SKEW * (len(lanes) - 1)):
        for k in range(len(lanes)):
            s = t - FFN_SKEW * k
            if 0 <= s < n_slots:
                slot(k, s)
                if s == n_slots - 1:
                    finished.append(k)
    if after_lane is not None:
        for k in finished:
            after_lane(lanes[k])


def _ffn1_kernel(x_ref, wg_ref, wu_ref, wd_ref, g_ref, b_ref, win_ref, wvt_ref,
                 x1_ref, p_ref, vt_ref, xb_ref, acc_ref):
    xb_ref[...] = x_ref[...].astype(BF16)

    def project(rows):
        x1 = _layer_norm(ALPHA * x_ref[rows, :] + 0.5 * acc_ref[rows, :],
                         g_ref[...], b_ref[...])
        x1_ref[rows, :] = x1
        x1b = x1.astype(BF16)
        group_scale = (NA_DH ** -0.5 * LOG2E, None, None, DF_DH ** -0.5 * LOG2E, None)
        for j, gs in enumerate(group_scale):
            cols = slice(j * NA_WIDTH, (j + 1) * NA_WIDTH)
            pj = jnp.dot(x1b, win_ref[:, cols], preferred_element_type=F32)
            if gs is not None:
                pj = pj * gs
            p_ref[rows, cols] = pj.astype(BF16)
        vt_ref[:, rows] = lax.dot_general(wvt_ref[...], x1b, (((1,), (1,)), ((), ())),
                                          preferred_element_type=F32).astype(BF16)

    _swiglu_lanes(xb_ref, wg_ref, wu_ref, wd_ref, acc_ref, after_lane=project)


def _ffn2_kernel(x1_ref, na_ref, df_ref, wout_ref, g2_ref, b2_ref,
                 wg_ref, wu_ref, wd_ref, g3_ref, b3_ref,
                 o_ref, x2_ref, xb_ref, acc_ref):
    def mix_in(rows):
        mix = jnp.dot(na_ref[rows, :], wout_ref[:NA_WIDTH, :], preferred_element_type=F32)
        mix += jnp.dot(df_ref[rows, :], wout_ref[NA_WIDTH:, :], preferred_element_type=F32)
        x2 = _layer_norm(ALPHA * x1_ref[rows, :] + mix, g2_ref[...], b2_ref[...])
        x2_ref[rows, :] = x2
        xb_ref[rows, :] = x2.astype(BF16)

    def finish(rows):
        o_ref[rows, :] = _layer_norm(ALPHA * x2_ref[rows, :] + 0.5 * acc_ref[rows, :],
                                     g3_ref[...], b3_ref[...])

    _swiglu_lanes(xb_ref, wg_ref, wu_ref, wd_ref, acc_ref, before_lane=mix_in,
                  after_lane=finish)


def _na_kernel(q_ref, k_ref, v_ref, tbl_ref, o_ref):
    rows_total = k_ref.shape[0] // GRID_W
    win_keys = NA_ROWS * GRID_W
    lane = lax.broadcasted_iota(jnp.int32, (GRID_W, LANES), 1)
    lo = lane < NA_DH

    def window(rr):
        r = pl.program_id(1) * NA_ROWS_PER_STEP + rr
        r0 = jnp.clip(r - NA_ROWS // 2, 0, rows_total - NA_ROWS)
        return pl.ds(pl.multiple_of(r0 * GRID_W, GRID_W), win_keys), r - r0

    def scores(rr):
        krows, e = window(rr)
        out = []
        for hp in range(NA_HEADS // 2):
            cols = slice(hp * LANES, (hp + 1) * LANES)
            q = q_ref[rr * GRID_W:(rr + 1) * GRID_W, cols]
            zero = jnp.zeros_like(q)
            qz = jnp.concatenate([jnp.where(lo, q, zero), jnp.where(lo, zero, q)], axis=0)
            s = lax.dot_general(qz, k_ref[krows, cols], (((1,), (1,)), ((), ())),
                                preferred_element_type=F32)
            bias = jnp.concatenate([jnp.concatenate(
                [tbl_ref[2 * hp + par, 2 * w - e + NA_ROWS - 1] for w in range(NA_ROWS // 2)],
                axis=1) for par in range(2)], axis=0)
            out.append(s + bias)
        return out

    def finish(rr, s_list):
        krows, _ = window(rr)
        for hp, s in enumerate(s_list):
            cols = slice(hp * LANES, (hp + 1) * LANES)
            m = jnp.max(s, axis=-1, keepdims=True)
            p = jnp.exp2(s - m)
            l = jnp.sum(p, axis=-1, keepdims=True)
            o = jnp.dot(p.astype(BF16), v_ref[krows, cols], preferred_element_type=F32)
            o = o / l
            o_ref[rr * GRID_W:(rr + 1) * GRID_W, cols] = jnp.where(
                lo, o[:GRID_W], o[GRID_W:]).astype(o_ref.dtype)

    pending = scores(0)
    for rr in range(NA_ROWS_PER_STEP):
        upcoming = scores(rr + 1) if rr + 1 < NA_ROWS_PER_STEP else None
        finish(rr, pending)
        pending = upcoming


def _split3(x):
    hi = x.astype(BF16).astype(F32)
    mid = (x - hi).astype(BF16).astype(F32)
    return hi, mid, (x - hi - mid).astype(BF16).astype(F32)


def _diff_kernel(slope_ref, q_ref, k_ref, vt_in_ref, lq1_ref, lk1_ref, lq2_ref, lk2_ref,
                 g_ref, o_ref, kp_ref, vt_ref, qv_ref, diag_ref, t_ref, *acc_refs):
    tq, tk = DF_TQ, DF_TK
    n_chunks = k_ref.shape[0] // tk
    ratio = tk // tq
    h = pl.program_id(1)
    i = pl.program_id(2)
    a = slope_ref[h] * LOG2E
    lane = lax.broadcasted_iota(jnp.int32, (tq, LANES), 1)

    @pl.when(i == 0)
    def _per_head_setup():
        t_keys = k_ref.shape[0]
        kp_ref[:, :LANES] = k_ref[...]
        klane = lax.broadcasted_iota(jnp.int32, (t_keys, LANES), 1)
        kpos = lax.broadcasted_iota(jnp.int32, (t_keys, LANES), 0).astype(F32)
        hi, mid, lo = _split3(a * kpos)
        feat = jnp.where(klane == 3, hi, jnp.where(klane == 4, mid, jnp.where(
            klane == 5, lo, 0.0)))
        kp_ref[:, LANES:] = jnp.where(klane < 3, 1.0, feat).astype(BF16)
        for j in range(n_chunks):
            vt_ref[j] = vt_in_ref[:, j * tk:(j + 1) * tk]
        rel = (lax.broadcasted_iota(jnp.int32, (tk, tq), 1)
               - lax.broadcasted_iota(jnp.int32, (tk, tq), 0)).astype(F32)
        for par in range(ratio):
            diag_ref[par] = -a * jnp.abs(rel + float(par * tq))

    block_info = []
    for blk in range(DF_BLOCKS):
        ib = i * DF_BLOCKS + blk
        q = q_ref[blk * tq:(blk + 1) * tq, :]
        zero = jnp.zeros_like(q)
        qpos = (ib * tq + lax.broadcasted_iota(jnp.int32, (tq, LANES), 0)).astype(F32)
        hi, mid, lo = _split3(-a * qpos)
        before = jnp.where(lane == 0, hi, jnp.where(lane == 1, mid, jnp.where(
            lane == 2, lo, jnp.where(lane < 6, 1.0, 0.0)))).astype(BF16)
        for half in range(2):
            qz = jnp.where(lane < DF_DH, q, zero) if half == 0 else jnp.where(
                lane < DF_DH, zero, q)
            rows = slice(half * tq, (half + 1) * tq)
            for kind, feat in enumerate((before, -before, zero)):
                qv_ref[blk, kind, rows, :LANES] = qz
                qv_ref[blk, kind, rows, LANES:] = feat
        block_info.append((lax.div(ib, ratio), lax.rem(ib, ratio)))

    def chunk_index(blk, d):
        return lax.rem(block_info[blk][0] + d, n_chunks)

    def sublane_groups(x):
        return [x[r:r + SUBLANES, :] for r in range(0, x.shape[0], SUBLANES)]

    def fold(op, accs, xs):
        accs = list(accs)
        for g, x in enumerate(xs):
            accs[g % len(accs)] = op(accs[g % len(accs)], x)
        return accs

    def merged(op, accs):
        out = accs[0]
        for x in accs[1:]:
            out = op(out, x)
        return out

    def score_chunk(blk, d, mx):
        j_diag, par = block_info[blk]
        j = chunk_index(blk, d)
        krows = pl.ds(pl.multiple_of(j * tk, tk), tk)
        kind = 2 if d == 0 else jnp.where(j < j_diag, 0, 1)
        t = lax.dot_general(kp_ref[krows, :], qv_ref[blk, kind], (((1,), (1,)), ((), ())),
                            preferred_element_type=F32)
        if d == 0:
            tile = diag_ref[par]
            t = t + jnp.concatenate([tile, tile], axis=1)
        t_ref[blk % DF_RING, d] = t
        return fold(jnp.maximum, mx, sublane_groups(t))

    def value_chunk(blk, d, m, lsum):
        pg = [jnp.exp2(x - m) for x in sublane_groups(t_ref[blk % DF_RING, d])]
        lsum = fold(jnp.add, lsum, pg)
        p = jnp.concatenate(pg, axis=0).astype(BF16)
        pv = jnp.dot(vt_ref[chunk_index(blk, d)], p, preferred_element_type=F32)
        acc_ref = acc_refs[blk % DF_STREAMS]
        if d == 0:
            acc_ref[...] = pv
        else:
            acc_ref[...] += pv
        return lsum

    lam = (jnp.exp(jnp.sum(lq1_ref[...] * lk1_ref[...], axis=-1, keepdims=True))
           - jnp.exp(jnp.sum(lq2_ref[...] * lk2_ref[...], axis=-1, keepdims=True))
           + LAM_INIT)

    neg = [jnp.full((SUBLANES, 2 * tq), NEG_BIG, F32)] * DF_STAT_ACCS
    zeros = [jnp.zeros((SUBLANES, 2 * tq), F32)] * DF_STAT_ACCS
    mx, m, lsum = {}, {}, {}

    def emit_value(blk, d):
        if d == 0:
            m[blk] = jnp.broadcast_to(
                jnp.max(merged(jnp.maximum, mx.pop(blk)), axis=0, keepdims=True),
                (SUBLANES, 2 * tq))
            lsum[blk] = zeros
        lsum[blk] = value_chunk(blk, d, m[blk], lsum[blk])
        if d == n_chunks - 1:
            col_sum = jnp.sum(merged(jnp.add, lsum.pop(blk)), axis=0, keepdims=True)
            o_t = acc_refs[blk % DF_STREAMS][...] / col_sum
            o = (o_t[:, :tq] - lam * o_t[:, tq:]).T
            y = o * lax.rsqrt(jnp.mean(o * o, axis=-1, keepdims=True) + LN_EPS)
            o_ref[blk * tq:(blk + 1) * tq, :] = (
                y * g_ref[...] * (1.0 - LAM_INIT)).astype(o_ref.dtype)

    def emit_score(blk, d):
        mx[blk] = score_chunk(blk, d, mx.get(blk, neg))

    def stream_ops(blocks):
        for d in range(n_chunks):
            yield [(emit_score, blocks[0], d)]
        for cur, nxt in zip(blocks, blocks[1:] + [None]):
            for d in range(n_chunks):
                ops = [(emit_value, cur, d)]
                if nxt is not None:
                    ops.append((emit_score, nxt, d))
                yield ops

    streams = [list(stream_ops(list(range(k, DF_BLOCKS, DF_STREAMS))))
               for k in range(DF_STREAMS)]
    offset = n_chunks // DF_STREAMS
    for step in range(max(len(ops) + k * offset for k, ops in enumerate(streams))):
        for k, ops in enumerate(streams):
            if 0 <= step - k * offset < len(ops):
                for fn, blk, d in ops[step - k * offset]:
                    fn(blk, d)


def _na_table_kernel(rpb_ref, o_ref):
    head = pl.program_id(0)
    n_dr, n_dc = 2 * NA_ROWS - 1, 2 * NA_COLS - 1
    c = lax.broadcasted_iota(jnp.int32, (GRID_W, LANES), 0)
    lane = lax.broadcasted_iota(jnp.int32, (GRID_W, LANES), 1)
    cc = lane & (GRID_W - 1)
    c0 = jnp.clip(c - NA_COLS // 2, 0, GRID_W - NA_COLS)
    in_window = (cc >= c0) & (cc < c0 + NA_COLS)
    dc = jnp.where(in_window, cc - c + (NA_COLS - 1), -1)
    upper = lane >= GRID_W
    for d in range(n_dr - 1):
        base = (head * n_dr + d) * n_dc
        acc = jnp.full((GRID_W, LANES), NEG_BIG, F32)
        for x in range(n_dc):
            val = jnp.where(upper, rpb_ref[base + n_dc + x], rpb_ref[base + x])
            acc = jnp.where(dc == x, val, acc)
        o_ref[0, d] = acc * LOG2E


def _na_bias_table(rpb):
    shape = (NA_HEADS, 2 * NA_ROWS - 2, GRID_W, LANES)
    return pl.pallas_call(
        _na_table_kernel,
        grid=(NA_HEADS,),
        in_specs=[pl.BlockSpec(memory_space=pltpu.SMEM)],
        out_specs=pl.BlockSpec((1, *shape[1:]), lambda h: (h, 0, 0, 0)),
        out_shape=jax.ShapeDtypeStruct(shape, F32),
        compiler_params=_params(1),
        name="na_bias_table",
    )(rpb.reshape(-1))


def _resident(shape):
    return pl.BlockSpec(shape, lambda *_: (0,) * len(shape), pipeline_mode=pl.Buffered(1))


def _params(n_axes):
    return pltpu.CompilerParams(dimension_semantics=("arbitrary",) * n_axes,
                                vmem_limit_bytes=VMEM_LIMIT)


def _ffn_weight_specs():
    return [_resident((D_MODEL, D_FF)), _resident((D_MODEL, D_FF)), _resident((D_FF, D_MODEL))]


def _ffn1_call(x2d, wg, wu, wd, g, b, w_in, w_vt):
    n = x2d.shape[0]
    row_spec = pl.BlockSpec((FFN_ROWS, D_MODEL), lambda i: (i, 0))
    vec_spec = _resident((1, D_MODEL))
    return pl.pallas_call(
        _ffn1_kernel,
        grid=(n // FFN_ROWS,),
        in_specs=[row_spec, *_ffn_weight_specs(), vec_spec, vec_spec,
                  _resident((D_MODEL, P_COLS)), _resident((DF_WIDTH, D_MODEL))],
        out_specs=[row_spec, pl.BlockSpec((FFN_ROWS, P_COLS), lambda i: (i, 0)),
                   pl.BlockSpec((DF_WIDTH, FFN_ROWS), lambda i: (0, i))],
        out_shape=[jax.ShapeDtypeStruct((n, D_MODEL), F32),
                   jax.ShapeDtypeStruct((n, P_COLS), BF16),
                   jax.ShapeDtypeStruct((DF_WIDTH, n), BF16)],
        scratch_shapes=[pltpu.VMEM((FFN_ROWS, D_MODEL), BF16),
                        pltpu.VMEM((FFN_ROWS, D_MODEL), F32)],
        compiler_params=_params(1),
        name="ffn1_ln1_qkv",
    )(x2d, wg, wu, wd, g, b, w_in, w_vt)


def _ffn2_call(x1, na_out, df_out, w_out, g2, b2, wg, wu, wd, g3, b3):
    n = x1.shape[0]
    row_spec = pl.BlockSpec((FFN_ROWS, D_MODEL), lambda i: (i, 0))
    half_spec = pl.BlockSpec((FFN_ROWS, NA_WIDTH), lambda i: (i, 0))
    vec_spec = _resident((1, D_MODEL))
    return pl.pallas_call(
        _ffn2_kernel,
        grid=(n // FFN_ROWS,),
        in_specs=[row_spec, half_spec, half_spec, _resident((D_MODEL, D_MODEL)),
                  vec_spec, vec_spec, *_ffn_weight_specs(), vec_spec, vec_spec],
        out_specs=row_spec,
        out_shape=jax.ShapeDtypeStruct((n, D_MODEL), F32),
        scratch_shapes=[pltpu.VMEM((FFN_ROWS, D_MODEL), F32),
                        pltpu.VMEM((FFN_ROWS, D_MODEL), BF16),
                        pltpu.VMEM((FFN_ROWS, D_MODEL), F32)],
        compiler_params=_params(1),
        name="outproj_ln2_ffn2_ln3",
    )(x1, na_out, df_out, w_out, g2, b2, wg, wu, wd, g3, b3)


def _na_call(p, tbl, batch, seq):
    step_rows = NA_ROWS_PER_STEP * GRID_W
    steps = seq // step_rows
    q_spec = pl.BlockSpec((step_rows, NA_WIDTH), lambda b, i: (b * steps + i, 0))
    return pl.pallas_call(
        _na_kernel,
        grid=(batch, steps),
        in_specs=[q_spec,
                  pl.BlockSpec((seq, NA_WIDTH), lambda b, i: (b, 1)),
                  pl.BlockSpec((seq, NA_WIDTH), lambda b, i: (b, 2)),
                  _resident(tbl.shape)],
        out_specs=q_spec,
        out_shape=jax.ShapeDtypeStruct((batch * seq, NA_WIDTH), BF16),
        compiler_params=_params(2),
        name="neighbourhood_attention",
    )(p, p, p, tbl)


def _diff_call(p, vt, slopes, lq1, lk1, lq2, lk2, subln_g, batch, seq):
    step_rows = DF_BLOCKS * DF_TQ
    steps = seq // step_rows
    q_col0 = 3 * NA_WIDTH // LANES
    k_col0 = q_col0 + DF_WIDTH // LANES
    lam_spec = _resident((1, DF_DH))
    return pl.pallas_call(
        _diff_kernel,
        grid=(batch, DF_HEADS, steps),
        in_specs=[pl.BlockSpec(memory_space=pltpu.SMEM),
                  pl.BlockSpec((step_rows, LANES), lambda b, h, i: (b * steps + i, q_col0 + h)),
                  pl.BlockSpec((seq, LANES), lambda b, h, i: (b, k_col0 + h)),
                  pl.BlockSpec((LANES, seq), lambda b, h, i: (h, b)),
                  lam_spec, lam_spec, lam_spec, lam_spec,
                  _resident((1, 2 * DF_DH))],
        out_specs=pl.BlockSpec((step_rows, LANES), lambda b, h, i: (b * steps + i, h)),
        out_shape=jax.ShapeDtypeStruct((batch * seq, DF_WIDTH), BF16),
        scratch_shapes=[pltpu.VMEM((seq, 2 * LANES), BF16),
                        pltpu.VMEM((seq // DF_TK, LANES, DF_TK), BF16),
                        pltpu.VMEM((DF_BLOCKS, 3, 2 * DF_TQ, 2 * LANES), BF16),
                        pltpu.VMEM((DF_TK // DF_TQ, DF_TK, DF_TQ), F32),
                        pltpu.VMEM((DF_RING, seq // DF_TK, DF_TK, 2 * DF_TQ), F32),
                        *[pltpu.VMEM((LANES, 2 * DF_TQ), F32)] * DF_STREAMS],
        compiler_params=_params(3),
        name="differential_attention",
    )(slopes, p, p, vt, lq1, lk1, lq2, lk2, subln_g)


def kernel(x, ln1_g, ln1_b, ffn1_w_gate, ffn1_w_up, ffn1_w_down, w_in, na_rpb,
           diff_lambda_q1, diff_lambda_k1, diff_lambda_q2, diff_lambda_k2, diff_subln_g,
           w_out, ln2_g, ln2_b, ffn2_w_gate, ffn2_w_up, ffn2_w_down, ln3_g, ln3_b):
    batch, seq, d = x.shape
    assert d == D_MODEL and seq % (NA_ROWS_PER_STEP * GRID_W) == 0
    assert seq % DF_TK == 0 and DF_TK % DF_TQ == 0 and seq % (DF_BLOCKS * DF_TQ) == 0
    assert ln1_g.shape[0] == DEPTH == 1
    x2d = x.reshape(batch * seq, d)
    x1, p, vt = _ffn1_call(x2d, ffn1_w_gate[0].astype(BF16), ffn1_w_up[0].astype(BF16),
                           ffn1_w_down[0].astype(BF16), ln1_g, ln1_b,
                           w_in[0][:, :P_COLS].astype(BF16), w_in[0][:, P_COLS:].T.astype(BF16))
    na_out = _na_call(p, _na_bias_table(na_rpb[0]), batch, seq)
    slopes = jnp.exp2(-8.0 * jnp.arange(1, DF_HEADS + 1, dtype=F32) / DF_HEADS)
    df_out = _diff_call(p, vt, slopes, diff_lambda_q1, diff_lambda_k1, diff_lambda_q2,
                        diff_lambda_k2, diff_subln_g, batch, seq)
    out = _ffn2_call(x1, na_out, df_out, w_out[0].astype(BF16), ln2_g, ln2_b,
                     ffn2_w_gate[0].astype(BF16), ffn2_w_up[0].astype(BF16),
                     ffn2_w_down[0].astype(BF16), ln3_g, ln3_b)
    return out.reshape(batch, seq, d)
```

```python
import functools
import math

import jax
import jax.numpy as jnp
from jax import lax
from jax.experimental import pallas as pl
from jax.experimental.pallas import tpu as pltpu

F32 = jnp.float32
BF16 = jnp.bfloat16

D_MODEL = 1024
D_FF = 2816
DEPTH = 1
GRID_W = 64
NA_HEADS, NA_DH = 8, 64
NA_ROWS, NA_COLS = 8, 16
DF_HEADS, DF_DH = 4, 64
NA_WIDTH = NA_HEADS * NA_DH
DF_WIDTH = DF_HEADS * 2 * DF_DH
IN_COLS = 3 * NA_WIDTH + 3 * DF_WIDTH
ALPHA = (2.0 * DEPTH) ** 0.25
LN_EPS = 1e-5
LAM_INIT = 0.8 - 0.6 * math.exp(-0.3 * 0)
NEG_BIG = -1e30
LOG2E = math.log2(math.e)

LANES = 128
MXU_DIM = 256
FF_CHUNK = MXU_DIM
N_FF_CHUNKS = D_FF // FF_CHUNK
FFN_ROWS = 512
FFN_SUB = 256
FFN_SKEW = 2
NA_ROWS_PER_STEP = 8
DF_TQ = 256
DF_TK = 512
DF_BLOCKS = 4
VMEM_LIMIT = 56 * 1024 * 1024


def _layer_norm(y, g, b):
    mu = jnp.mean(y, axis=-1, keepdims=True)
    yc = y - mu
    var = jnp.mean(yc * yc, axis=-1, keepdims=True)
    return yc * lax.rsqrt(var + LN_EPS) * g + b


def _swiglu_lanes(xb_ref, wg_ref, wu_ref, wd_ref, acc_ref, before_lane=None, after_lane=None):
    lanes = [slice(r, r + FFN_SUB) for r in range(0, FFN_ROWS, FFN_SUB)]
    pending = [None] * len(lanes)
    if before_lane is not None:
        for rows in lanes:
            before_lane(rows)

    def slot(k, s):
        rows = lanes[k]
        h_new = None
        if s < N_FF_CHUNKS:
            xb = xb_ref[rows, :]
            cols = slice(s * FF_CHUNK, (s + 1) * FF_CHUNK)
            g = jnp.dot(xb, wg_ref[:, cols], preferred_element_type=F32)
            u = jnp.dot(xb, wu_ref[:, cols], preferred_element_type=F32)
            h_new = (g * jax.nn.sigmoid(g) * u).astype(BF16)
        if s >= 1:
            d = jnp.dot(pending[k], wd_ref[(s - 1) * FF_CHUNK:s * FF_CHUNK, :],
                        preferred_element_type=F32)
            if s == 1:
                acc_ref[rows, :] = d
            else:
                acc_ref[rows, :] += d
        pending[k] = h_new

    n_slots = N_FF_CHUNKS + 1
    finished = []
    for t in range(n_slots + FFN_SKEW * (len(lanes) - 1)):
        for k in range(len(lanes)):
            s = t - FFN_SKEW * k
            if 0 <= s < n_slots:
                slot(k, s)
                if s == n_slots - 1:
                    finished.append(k)
    if after_lane is not None:
        for k in finished:
            after_lane(lanes[k])


def _ffn1_kernel(x_ref, wg_ref, wu_ref, wd_ref, g_ref, b_ref, win_ref,
                 x1_ref, p_ref, xb_ref, acc_ref):
    xb_ref[...] = x_ref[...].astype(BF16)

    def project(rows):
        x1 = _layer_norm(ALPHA * x_ref[rows, :] + 0.5 * acc_ref[rows, :],
                         g_ref[...], b_ref[...])
        x1_ref[rows, :] = x1
        x1b = x1.astype(BF16)
        group_scale = (NA_DH ** -0.5 * LOG2E, None, None, DF_DH ** -0.5 * LOG2E, None, None)
        for j, gs in enumerate(group_scale):
            cols = slice(j * NA_WIDTH, (j + 1) * NA_WIDTH)
            pj = jnp.dot(x1b, win_ref[:, cols], preferred_element_type=F32)
            if gs is not None:
                pj = pj * gs
            p_ref[rows, cols] = pj.astype(BF16)

    _swiglu_lanes(xb_ref, wg_ref, wu_ref, wd_ref, acc_ref, after_lane=project)


def _ffn2_kernel(x1_ref, na_ref, df_ref, wout_ref, g2_ref, b2_ref,
                 wg_ref, wu_ref, wd_ref, g3_ref, b3_ref,
                 o_ref, x2_ref, xb_ref, acc_ref):
    def mix_in(rows):
        mix = jnp.dot(na_ref[rows, :], wout_ref[:NA_WIDTH, :], preferred_element_type=F32)
        mix += jnp.dot(df_ref[rows, :], wout_ref[NA_WIDTH:, :], preferred_element_type=F32)
        x2 = _layer_norm(ALPHA * x1_ref[rows, :] + mix, g2_ref[...], b2_ref[...])
        x2_ref[rows, :] = x2
        xb_ref[rows, :] = x2.astype(BF16)

    def finish(rows):
        o_ref[rows, :] = _layer_norm(ALPHA * x2_ref[rows, :] + 0.5 * acc_ref[rows, :],
                                     g3_ref[...], b3_ref[...])

    _swiglu_lanes(xb_ref, wg_ref, wu_ref, wd_ref, acc_ref, before_lane=mix_in,
                  after_lane=finish)


def _na_kernel(q_ref, k_ref, v_ref, tbl_ref, o_ref):
    rows_total = k_ref.shape[0] // GRID_W
    win_keys = NA_ROWS * GRID_W
    lane = lax.broadcasted_iota(jnp.int32, (GRID_W, LANES), 1)
    lo = lane < NA_DH

    def window(rr):
        r = pl.program_id(1) * NA_ROWS_PER_STEP + rr
        r0 = jnp.clip(r - NA_ROWS // 2, 0, rows_total - NA_ROWS)
        return pl.ds(pl.multiple_of(r0 * GRID_W, GRID_W), win_keys), r - r0

    def scores(rr):
        krows, e = window(rr)
        out = []
        for hp in range(NA_HEADS // 2):
            cols = slice(hp * LANES, (hp + 1) * LANES)
            q = q_ref[rr * GRID_W:(rr + 1) * GRID_W, cols]
            zero = jnp.zeros_like(q)
            qz = jnp.concatenate([jnp.where(lo, q, zero), jnp.where(lo, zero, q)], axis=0)
            s = lax.dot_general(qz, k_ref[krows, cols], (((1,), (1,)), ((), ())),
                                preferred_element_type=F32)
            bias = jnp.concatenate([jnp.concatenate(
                [tbl_ref[2 * hp + par, 2 * w - e + NA_ROWS - 1] for w in range(NA_ROWS // 2)],
                axis=1) for par in range(2)], axis=0)
            out.append(s + bias)
        return out

    def finish(rr, s_list):
        krows, _ = window(rr)
        for hp, s in enumerate(s_list):
            cols = slice(hp * LANES, (hp + 1) * LANES)
            m = jnp.max(s, axis=-1, keepdims=True)
            p = jnp.exp2(s - m)
            l = jnp.sum(p, axis=-1, keepdims=True)
            o = jnp.dot(p.astype(BF16), v_ref[krows, cols], preferred_element_type=F32)
            o = o / l
            o_ref[rr * GRID_W:(rr + 1) * GRID_W, cols] = jnp.where(
                lo, o[:GRID_W], o[GRID_W:]).astype(o_ref.dtype)

    pending = scores(0)
    for rr in range(NA_ROWS_PER_STEP):
        upcoming = scores(rr + 1) if rr + 1 < NA_ROWS_PER_STEP else None
        finish(rr, pending)
        pending = upcoming


def _split3(x):
    hi = x.astype(BF16).astype(F32)
    mid = (x - hi).astype(BF16).astype(F32)
    return hi, mid, (x - hi - mid).astype(BF16).astype(F32)


def _diff_kernel(slope_ref, q_ref, k_ref, v_ref, lq1_ref, lk1_ref, lq2_ref, lk2_ref, g_ref,
                 o_ref, kp_ref, vp_ref, qv_ref, diag_ref, t_ref, mx_ref, acc_ref):
    tq, tk = DF_TQ, DF_TK
    n_chunks = k_ref.shape[0] // tk
    ratio = tk // tq
    groups = tk // LANES
    h = pl.program_id(1)
    i = pl.program_id(2)
    a = slope_ref[h] * LOG2E
    lane = lax.broadcasted_iota(jnp.int32, (tq, LANES), 1)

    @pl.when(i == 0)
    def _per_head_setup():
        t_keys = k_ref.shape[0]
        kp_ref[:, :LANES] = k_ref[...]
        klane = lax.broadcasted_iota(jnp.int32, (t_keys, LANES), 1)
        kpos = lax.broadcasted_iota(jnp.int32, (t_keys, LANES), 0).astype(F32)
        hi, mid, lo = _split3(a * kpos)
        feat = jnp.where(klane == 3, hi, jnp.where(klane == 4, mid, jnp.where(
            klane == 5, lo, 0.0)))
        kp_ref[:, LANES:] = jnp.where(klane < 3, 1.0, feat).astype(BF16)
        vp_ref[:, :LANES] = v_ref[...]
        vp_ref[:, LANES:] = jnp.where(klane == 0, 1.0, 0.0).astype(BF16)
        rel = (lax.broadcasted_iota(jnp.int32, (tq, tk), 0)
               - lax.broadcasted_iota(jnp.int32, (tq, tk), 1)).astype(F32)
        for par in range(ratio):
            diag_ref[par] = -a * jnp.abs(rel + float(par * tq))

    block_info = []
    for blk in range(DF_BLOCKS):
        ib = i * DF_BLOCKS + blk
        q = q_ref[blk * tq:(blk + 1) * tq, :]
        zero = jnp.zeros_like(q)
        qpos = (ib * tq + lax.broadcasted_iota(jnp.int32, (tq, LANES), 0)).astype(F32)
        hi, mid, lo = _split3(-a * qpos)
        before = jnp.where(lane == 0, hi, jnp.where(lane == 1, mid, jnp.where(
            lane == 2, lo, jnp.where(lane < 6, 1.0, 0.0)))).astype(BF16)
        for half in range(2):
            qz = jnp.where(lane < DF_DH, q, zero) if half == 0 else jnp.where(
                lane < DF_DH, zero, q)
            for kind, feat in enumerate((before, -before, zero)):
                qv_ref[blk, half, kind, :, :LANES] = qz
                qv_ref[blk, half, kind, :, LANES:] = feat
        block_info.append((lax.div(ib, ratio), lax.rem(ib, ratio)))

    def chunk_rows(blk, d):
        j_diag = block_info[blk][0]
        j = lax.rem(j_diag + d, n_chunks)
        return j, pl.ds(pl.multiple_of(j * tk, tk), tk)

    def lane_groups(x):
        return [x[:, g * LANES:(g + 1) * LANES] for g in range(groups)]

    def score_chunk(n, d):
        blk, half = divmod(n, 2)
        j_diag, par = block_info[blk]
        j, rows = chunk_rows(blk, d)
        dims = (((1,), (1,)), ((), ()))
        if d == 0:
            t = lax.dot_general(qv_ref[blk, half, 2], kp_ref[rows, :], dims,
                                preferred_element_type=F32) + diag_ref[par]
        else:
            kind = jnp.where(j < j_diag, 0, 1)
            t = lax.dot_general(qv_ref[blk, half, kind], kp_ref[rows, :], dims,
                                preferred_element_type=F32)
        t_ref[n % 2, :, d * tk:(d + 1) * tk] = t
        tg = lane_groups(t)
        while len(tg) > 1:
            tg = [jnp.maximum(x, y) for x, y in zip(tg[::2], tg[1::2])]
        mx_ref[n % 2] = tg[0] if d == 0 else jnp.maximum(mx_ref[n % 2], tg[0])

    def value_chunk(n, d, m):
        _, rows = chunk_rows(n // 2, d)
        pg = [jnp.exp2(x - m) for x in lane_groups(t_ref[n % 2, :, d * tk:(d + 1) * tk])]
        p = jnp.concatenate([x.astype(BF16) for x in pg], axis=1)
        pv = jnp.dot(p, vp_ref[rows, :], preferred_element_type=F32)
        if d == 0:
            acc_ref[n % 2] = pv
        else:
            acc_ref[n % 2] += pv

    n_sets = 2 * DF_BLOCKS
    for d in range(n_chunks):
        score_chunk(0, d)
    outs = []
    for n in range(n_sets):
        m = jnp.broadcast_to(jnp.max(mx_ref[n % 2], axis=-1, keepdims=True), (tq, LANES))
        for d in range(n_chunks):
            if n + 1 < n_sets:
                score_chunk(n + 1, d)
            value_chunk(n, d, m)
        outs.append(acc_ref[n % 2, :, :LANES] / acc_ref[n % 2, :, LANES:LANES + 1])

    lam = (jnp.exp(jnp.sum(lq1_ref[...] * lk1_ref[...], axis=-1, keepdims=True))
           - jnp.exp(jnp.sum(lq2_ref[...] * lk2_ref[...], axis=-1, keepdims=True))
           + LAM_INIT)
    for blk in range(DF_BLOCKS):
        o = outs[2 * blk] - lam * outs[2 * blk + 1]
        y = o * lax.rsqrt(jnp.mean(o * o, axis=-1, keepdims=True) + LN_EPS)
        o_ref[blk * tq:(blk + 1) * tq, :] = (y * g_ref[...] * (1.0 - LAM_INIT)).astype(
            o_ref.dtype)


def _na_table_kernel(rpb_ref, o_ref):
    head = pl.program_id(0)
    n_dr, n_dc = 2 * NA_ROWS - 1, 2 * NA_COLS - 1
    c = lax.broadcasted_iota(jnp.int32, (GRID_W, LANES), 0)
    lane = lax.broadcasted_iota(jnp.int32, (GRID_W, LANES), 1)
    cc = lane & (GRID_W - 1)
    c0 = jnp.clip(c - NA_COLS // 2, 0, GRID_W - NA_COLS)
    in_window = (cc >= c0) & (cc < c0 + NA_COLS)
    dc = jnp.where(in_window, cc - c + (NA_COLS - 1), -1)
    upper = lane >= GRID_W
    for d in range(n_dr - 1):
        base = (head * n_dr + d) * n_dc
        acc = jnp.full((GRID_W, LANES), NEG_BIG, F32)
        for x in range(n_dc):
            val = jnp.where(upper, rpb_ref[base + n_dc + x], rpb_ref[base + x])
            acc = jnp.where(dc == x, val, acc)
        o_ref[0, d] = acc * LOG2E


def _na_bias_table(rpb):
    shape = (NA_HEADS, 2 * NA_ROWS - 2, GRID_W, LANES)
    return pl.pallas_call(
        _na_table_kernel,
        grid=(NA_HEADS,),
        in_specs=[pl.BlockSpec(memory_space=pltpu.SMEM)],
        out_specs=pl.BlockSpec((1, *shape[1:]), lambda h: (h, 0, 0, 0)),
        out_shape=jax.ShapeDtypeStruct(shape, F32),
        compiler_params=_params(1),
        name="na_bias_table",
    )(rpb.reshape(-1))


def _resident(shape):
    return pl.BlockSpec(shape, lambda *_: (0,) * len(shape), pipeline_mode=pl.Buffered(1))


def _params(n_axes):
    return pltpu.CompilerParams(dimension_semantics=("arbitrary",) * n_axes,
                                vmem_limit_bytes=VMEM_LIMIT)


def _ffn_weight_specs():
    return [_resident((D_MODEL, D_FF)), _resident((D_MODEL, D_FF)), _resident((D_FF, D_MODEL))]


def _ffn1_call(x2d, wg, wu, wd, g, b, w_in):
    n = x2d.shape[0]
    row_spec = pl.BlockSpec((FFN_ROWS, D_MODEL), lambda i: (i, 0))
    vec_spec = _resident((1, D_MODEL))
    return pl.pallas_call(
        _ffn1_kernel,
        grid=(n // FFN_ROWS,),
        in_specs=[row_spec, *_ffn_weight_specs(), vec_spec, vec_spec,
                  _resident((D_MODEL, IN_COLS))],
        out_specs=[row_spec, pl.BlockSpec((FFN_ROWS, IN_COLS), lambda i: (i, 0))],
        out_shape=[jax.ShapeDtypeStruct((n, D_MODEL), F32),
                   jax.ShapeDtypeStruct((n, IN_COLS), BF16)],
        scratch_shapes=[pltpu.VMEM((FFN_ROWS, D_MODEL), BF16),
                        pltpu.VMEM((FFN_ROWS, D_MODEL), F32)],
        compiler_params=_params(1),
        name="ffn1_ln1_qkv",
    )(x2d, wg, wu, wd, g, b, w_in)


def _ffn2_call(x1, na_out, df_out, w_out, g2, b2, wg, wu, wd, g3, b3):
    n = x1.shape[0]
    row_spec = pl.BlockSpec((FFN_ROWS, D_MODEL), lambda i: (i, 0))
    half_spec = pl.BlockSpec((FFN_ROWS, NA_WIDTH), lambda i: (i, 0))
    vec_spec = _resident((1, D_MODEL))
    return pl.pallas_call(
        _ffn2_kernel,
        grid=(n // FFN_ROWS,),
        in_specs=[row_spec, half_spec, half_spec, _resident((D_MODEL, D_MODEL)),
                  vec_spec, vec_spec, *_ffn_weight_specs(), vec_spec, vec_spec],
        out_specs=row_spec,
        out_shape=jax.ShapeDtypeStruct((n, D_MODEL), F32),
        scratch_shapes=[pltpu.VMEM((FFN_ROWS, D_MODEL), F32),
                        pltpu.VMEM((FFN_ROWS, D_MODEL), BF16),
                        pltpu.VMEM((FFN_ROWS, D_MODEL), F32)],
        compiler_params=_params(1),
        name="outproj_ln2_ffn2_ln3",
    )(x1, na_out, df_out, w_out, g2, b2, wg, wu, wd, g3, b3)


def _na_call(p, tbl, batch, seq):
    step_rows = NA_ROWS_PER_STEP * GRID_W
    steps = seq // step_rows
    q_spec = pl.BlockSpec((step_rows, NA_WIDTH), lambda b, i: (b * steps + i, 0))
    return pl.pallas_call(
        _na_kernel,
        grid=(batch, steps),
        in_specs=[q_spec,
                  pl.BlockSpec((seq, NA_WIDTH), lambda b, i: (b, 1)),
                  pl.BlockSpec((seq, NA_WIDTH), lambda b, i: (b, 2)),
                  _resident(tbl.shape)],
        out_specs=q_spec,
        out_shape=jax.ShapeDtypeStruct((batch * seq, NA_WIDTH), BF16),
        compiler_params=_params(2),
        name="neighbourhood_attention",
    )(p, p, p, tbl)


def _diff_call(p, slopes, lq1, lk1, lq2, lk2, subln_g, batch, seq):
    step_rows = DF_BLOCKS * DF_TQ
    steps = seq // step_rows
    q_col0 = 3 * NA_WIDTH // LANES
    k_col0 = q_col0 + DF_WIDTH // LANES
    v_col0 = k_col0 + DF_WIDTH // LANES
    lam_spec = _resident((1, DF_DH))
    return pl.pallas_call(
        _diff_kernel,
        grid=(batch, DF_HEADS, steps),
        in_specs=[pl.BlockSpec(memory_space=pltpu.SMEM),
                  pl.BlockSpec((step_rows, LANES), lambda b, h, i: (b * steps + i, q_col0 + h)),
                  pl.BlockSpec((seq, LANES), lambda b, h, i: (b, k_col0 + h)),
                  pl.BlockSpec((seq, LANES), lambda b, h, i: (b, v_col0 + h)),
                  lam_spec, lam_spec, lam_spec, lam_spec,
                  _resident((1, 2 * DF_DH))],
        out_specs=pl.BlockSpec((step_rows, LANES), lambda b, h, i: (b * steps + i, h)),
        out_shape=jax.ShapeDtypeStruct((batch * seq, DF_WIDTH), BF16),
        scratch_shapes=[pltpu.VMEM((seq, 2 * LANES), BF16),
                        pltpu.VMEM((seq, 2 * LANES), BF16),
                        pltpu.VMEM((DF_BLOCKS, 2, 3, DF_TQ, 2 * LANES), BF16),
                        pltpu.VMEM((DF_TK // DF_TQ, DF_TQ, DF_TK), F32),
                        pltpu.VMEM((2, DF_TQ, seq), F32),
                        pltpu.VMEM((2, DF_TQ, LANES), F32),
                        pltpu.VMEM((2, DF_TQ, 2 * LANES), F32)],
        compiler_params=_params(3),
        name="differential_attention",
    )(slopes, p, p, p, lq1, lk1, lq2, lk2, subln_g)


def kernel(x, ln1_g, ln1_b, ffn1_w_gate, ffn1_w_up, ffn1_w_down, w_in, na_rpb,
           diff_lambda_q1, diff_lambda_k1, diff_lambda_q2, diff_lambda_k2, diff_subln_g,
           w_out, ln2_g, ln2_b, ffn2_w_gate, ffn2_w_up, ffn2_w_down, ln3_g, ln3_b):
    batch, seq, d = x.shape
    assert d == D_MODEL and seq % (NA_ROWS_PER_STEP * GRID_W) == 0
    assert seq % DF_TK == 0 and DF_TK % DF_TQ == 0 and seq % (DF_BLOCKS * DF_TQ) == 0
    assert ln1_g.shape[0] == DEPTH == 1
    x2d = x.reshape(batch * seq, d)
    x1, p = _ffn1_call(x2d, ffn1_w_gate[0].astype(BF16), ffn1_w_up[0].astype(BF16),
                       ffn1_w_down[0].astype(BF16), ln1_g, ln1_b, w_in[0].astype(BF16))
    na_out = _na_call(p, _na_bias_table(na_rpb[0]), batch, seq)
    slopes = jnp.exp2(-8.0 * jnp.arange(1, DF_HEADS + 1, dtype=F32) / DF_HEADS)
    df_out = _diff_call(p, slopes, diff_lambda_q1, diff_lambda_k1, diff_lambda_q2,
                        diff_lambda_k2, diff_subln_g, batch, seq)
    out = _ffn2_call(x1, na_out, df_out, w_out[0].astype(BF16), ln2_g, ln2_b,
                     ffn2_w_gate[0].astype(BF16), ffn2_w_up[0].astype(BF16),
                     ffn2_w_down[0].astype(BF16), ln3_g, ln3_b)
    return out.reshape(batch, seq, d)
```

```python
import functools
import math

import jax
import jax.numpy as jnp
from jax import lax
from jax.experimental import pallas as pl
from jax.experimental.pallas import tpu as pltpu

F32 = jnp.float32
BF16 = jnp.bfloat16

D_MODEL = 1024
D_FF = 2816
DEPTH = 1
GRID_W = 64
NA_HEADS, NA_DH = 8, 64
NA_ROWS, NA_COLS = 8, 16
DF_HEADS, DF_DH = 4, 64
NA_WIDTH = NA_HEADS * NA_DH
DF_WIDTH = DF_HEADS * 2 * DF_DH
IN_COLS = 3 * NA_WIDTH + 3 * DF_WIDTH
ALPHA = (2.0 * DEPTH) ** 0.25
LN_EPS = 1e-5
LAM_INIT = 0.8 - 0.6 * math.exp(-0.3 * 0)
NEG_BIG = -1e30
LOG2E = math.log2(math.e)

LANES = 128
MXU_DIM = 256
FF_CHUNK = MXU_DIM
N_FF_CHUNKS = D_FF // FF_CHUNK
FFN_ROWS = 512
FFN_SUB = 256
FFN_SKEW = 2
NA_ROWS_PER_STEP = 16
DF_TQ = 256
DF_TK = 512
DF_BLOCKS = 8
VMEM_LIMIT = 56 * 1024 * 1024


def _layer_norm(y, g, b):
    mu = jnp.mean(y, axis=-1, keepdims=True)
    yc = y - mu
    var = jnp.mean(yc * yc, axis=-1, keepdims=True)
    return yc * lax.rsqrt(var + LN_EPS) * g + b


def _swiglu_lanes(xb_ref, wg_ref, wu_ref, wd_ref, acc_ref, before_lane=None, after_lane=None):
    lanes = [slice(r, r + FFN_SUB) for r in range(0, FFN_ROWS, FFN_SUB)]
    pending = [None] * len(lanes)
    if before_lane is not None:
        for rows in lanes:
            before_lane(rows)

    def slot(k, s):
        rows = lanes[k]
        h_new = None
        if s < N_FF_CHUNKS:
            xb = xb_ref[rows, :]
            cols = slice(s * FF_CHUNK, (s + 1) * FF_CHUNK)
            g = jnp.dot(xb, wg_ref[:, cols], preferred_element_type=F32)
            u = jnp.dot(xb, wu_ref[:, cols], preferred_element_type=F32)
            h_new = (g * jax.nn.sigmoid(g) * u).astype(BF16)
        if s >= 1:
            d = jnp.dot(pending[k], wd_ref[(s - 1) * FF_CHUNK:s * FF_CHUNK, :],
                        preferred_element_type=F32)
            if s == 1:
                acc_ref[rows, :] = d
            else:
                acc_ref[rows, :] += d
        pending[k] = h_new

    n_slots = N_FF_CHUNKS + 1
    finished = []
    for t in range(n_slots + FFN_SKEW * (len(lanes) - 1)):
        for k in range(len(lanes)):
            s = t - FFN_SKEW * k
            if 0 <= s < n_slots:
                slot(k, s)
                if s == n_slots - 1:
                    finished.append(k)
    if after_lane is not None:
        for k in finished:
            after_lane(lanes[k])


def _ffn1_kernel(x_ref, wg_ref, wu_ref, wd_ref, g_ref, b_ref, win_ref,
                 x1_ref, p_ref, xb_ref, acc_ref):
    xb_ref[...] = x_ref[...].astype(BF16)

    def project(rows):
        x1 = _layer_norm(ALPHA * x_ref[rows, :] + 0.5 * acc_ref[rows, :],
                         g_ref[...], b_ref[...])
        x1_ref[rows, :] = x1
        x1b = x1.astype(BF16)
        group_scale = (NA_DH ** -0.5 * LOG2E, None, None, DF_DH ** -0.5 * LOG2E, None, None)
        for j, gs in enumerate(group_scale):
            cols = slice(j * NA_WIDTH, (j + 1) * NA_WIDTH)
            pj = jnp.dot(x1b, win_ref[:, cols], preferred_element_type=F32)
            if gs is not None:
                pj = pj * gs
            p_ref[rows, cols] = pj.astype(BF16)

    _swiglu_lanes(xb_ref, wg_ref, wu_ref, wd_ref, acc_ref, after_lane=project)


def _ffn2_kernel(x1_ref, na_ref, df_ref, wout_ref, g2_ref, b2_ref,
                 wg_ref, wu_ref, wd_ref, g3_ref, b3_ref,
                 o_ref, x2_ref, xb_ref, acc_ref):
    def mix_in(rows):
        mix = jnp.dot(na_ref[rows, :], wout_ref[:NA_WIDTH, :], preferred_element_type=F32)
        mix += jnp.dot(df_ref[rows, :], wout_ref[NA_WIDTH:, :], preferred_element_type=F32)
        x2 = _layer_norm(ALPHA * x1_ref[rows, :] + mix, g2_ref[...], b2_ref[...])
        x2_ref[rows, :] = x2
        xb_ref[rows, :] = x2.astype(BF16)

    def finish(rows):
        o_ref[rows, :] = _layer_norm(ALPHA * x2_ref[rows, :] + 0.5 * acc_ref[rows, :],
                                     g3_ref[...], b3_ref[...])

    _swiglu_lanes(xb_ref, wg_ref, wu_ref, wd_ref, acc_ref, before_lane=mix_in,
                  after_lane=finish)


def _na_kernel(q_ref, k_ref, v_ref, tbl_ref, o_ref):
    rows_total = k_ref.shape[0] // GRID_W
    win_keys = NA_ROWS * GRID_W
    lane = lax.broadcasted_iota(jnp.int32, (GRID_W, LANES), 1)
    lo = lane < NA_DH

    def window(rr):
        r = pl.program_id(1) * NA_ROWS_PER_STEP + rr
        r0 = jnp.clip(r - NA_ROWS // 2, 0, rows_total - NA_ROWS)
        return pl.ds(pl.multiple_of(r0 * GRID_W, GRID_W), win_keys), r - r0

    def scores(rr):
        krows, e = window(rr)
        out = []
        for hp in range(NA_HEADS // 2):
            cols = slice(hp * LANES, (hp + 1) * LANES)
            q = q_ref[rr * GRID_W:(rr + 1) * GRID_W, cols]
            zero = jnp.zeros_like(q)
            qz = jnp.concatenate([jnp.where(lo, q, zero), jnp.where(lo, zero, q)], axis=0)
            s = lax.dot_general(qz, k_ref[krows, cols], (((1,), (1,)), ((), ())),
                                preferred_element_type=F32)
            bias = jnp.concatenate([jnp.concatenate(
                [tbl_ref[2 * hp + par, 2 * w - e + NA_ROWS - 1] for w in range(NA_ROWS // 2)],
                axis=1) for par in range(2)], axis=0)
            out.append(s + bias)
        return out

    def finish(rr, s_list):
        krows, _ = window(rr)
        for hp, s in enumerate(s_list):
            cols = slice(hp * LANES, (hp + 1) * LANES)
            m = jnp.max(s, axis=-1, keepdims=True)
            p = jnp.exp2(s - m)
            l = jnp.sum(p, axis=-1, keepdims=True)
            o = jnp.dot(p.astype(BF16), v_ref[krows, cols], preferred_element_type=F32)
            o = o / l
            o_ref[rr * GRID_W:(rr + 1) * GRID_W, cols] = jnp.where(
                lo, o[:GRID_W], o[GRID_W:]).astype(o_ref.dtype)

    pending = scores(0)
    for rr in range(NA_ROWS_PER_STEP):
        upcoming = scores(rr + 1) if rr + 1 < NA_ROWS_PER_STEP else None
        finish(rr, pending)
        pending = upcoming


def _split3(x):
    hi = x.astype(BF16).astype(F32)
    mid = (x - hi).astype(BF16).astype(F32)
    return hi, mid, (x - hi - mid).astype(BF16).astype(F32)


def _diff_kernel(slope_ref, q_ref, k_ref, v_ref, lq1_ref, lk1_ref, lq2_ref, lk2_ref, g_ref,
                 o_ref, kp_ref, vp_ref, qv_ref, diag_ref, t_ref, mx_ref, acc_ref):
    tq, tk = DF_TQ, DF_TK
    n_chunks = k_ref.shape[0] // tk
    ratio = tk // tq
    groups = tk // LANES
    h = pl.program_id(0)
    b = pl.program_id(1)
    i = pl.program_id(2)
    a = slope_ref[h] * LOG2E
    lane = lax.broadcasted_iota(jnp.int32, (tq, LANES), 1)

    @pl.when((i == 0) & (b == 0))
    def _per_head_setup():
        t_keys = k_ref.shape[0]
        klane = lax.broadcasted_iota(jnp.int32, (t_keys, LANES), 1)
        kpos = lax.broadcasted_iota(jnp.int32, (t_keys, LANES), 0).astype(F32)
        hi, mid, lo = _split3(a * kpos)
        feat = jnp.where(klane == 3, hi, jnp.where(klane == 4, mid, jnp.where(
            klane == 5, lo, 0.0)))
        kp_ref[:, LANES:] = jnp.where(klane < 3, 1.0, feat).astype(BF16)
        vp_ref[:, LANES:] = jnp.where(klane == 0, 1.0, 0.0).astype(BF16)
        rel = (lax.broadcasted_iota(jnp.int32, (tq, tk), 0)
               - lax.broadcasted_iota(jnp.int32, (tq, tk), 1)).astype(F32)
        for par in range(ratio):
            diag_ref[par] = -a * jnp.abs(rel + float(par * tq))

    @pl.when(i == 0)
    def _per_sequence_setup():
        kp_ref[:, :LANES] = k_ref[...]
        vp_ref[:, :LANES] = v_ref[...]

    block_info = []
    for blk in range(DF_BLOCKS):
        ib = i * DF_BLOCKS + blk
        q = q_ref[blk * tq:(blk + 1) * tq, :]
        zero = jnp.zeros_like(q)
        qpos = (ib * tq + lax.broadcasted_iota(jnp.int32, (tq, LANES), 0)).astype(F32)
        hi, mid, lo = _split3(-a * qpos)
        before = jnp.where(lane == 0, hi, jnp.where(lane == 1, mid, jnp.where(
            lane == 2, lo, jnp.where(lane < 6, 1.0, 0.0)))).astype(BF16)
        for half in range(2):
            qz = jnp.where(lane < DF_DH, q, zero) if half == 0 else jnp.where(
                lane < DF_DH, zero, q)
            for kind, feat in enumerate((before, -before, zero)):
                qv_ref[blk, half, kind, :, :LANES] = qz
                qv_ref[blk, half, kind, :, LANES:] = feat
        block_info.append((lax.div(ib, ratio), lax.rem(ib, ratio)))

    def chunk_rows(blk, d):
        j_diag = block_info[blk][0]
        j = lax.rem(j_diag + d, n_chunks)
        return j, pl.ds(pl.multiple_of(j * tk, tk), tk)

    def lane_groups(x):
        return [x[:, g * LANES:(g + 1) * LANES] for g in range(groups)]

    def score_chunk(n, d):
        blk, half = divmod(n, 2)
        j_diag, par = block_info[blk]
        j, rows = chunk_rows(blk, d)
        dims = (((1,), (1,)), ((), ()))
        if d == 0:
            t = lax.dot_general(qv_ref[blk, half, 2], kp_ref[rows, :], dims,
                                preferred_element_type=F32) + diag_ref[par]
        else:
            kind = jnp.where(j < j_diag, 0, 1)
            t = lax.dot_general(qv_ref[blk, half, kind], kp_ref[rows, :], dims,
                                preferred_element_type=F32)
        t_ref[n % 2, :, d * tk:(d + 1) * tk] = t
        tg = lane_groups(t)
        while len(tg) > 1:
            tg = [jnp.maximum(x, y) for x, y in zip(tg[::2], tg[1::2])]
        mx_ref[n % 2] = tg[0] if d == 0 else jnp.maximum(mx_ref[n % 2], tg[0])

    def value_chunk(n, d, m):
        _, rows = chunk_rows(n // 2, d)
        pg = [jnp.exp2(x - m) for x in lane_groups(t_ref[n % 2, :, d * tk:(d + 1) * tk])]
        p = jnp.concatenate([x.astype(BF16) for x in pg], axis=1)
        pv = jnp.dot(p, vp_ref[rows, :], preferred_element_type=F32)
        if d == 0:
            acc_ref[n % 2] = pv
        else:
            acc_ref[n % 2] += pv

    n_sets = 2 * DF_BLOCKS
    for d in range(n_chunks):
        score_chunk(0, d)
    outs = []
    for n in range(n_sets):
        m = jnp.broadcast_to(jnp.max(mx_ref[n % 2], axis=-1, keepdims=True), (tq, LANES))
        for d in range(n_chunks):
            if n + 1 < n_sets:
                score_chunk(n + 1, d)
            value_chunk(n, d, m)
        outs.append(acc_ref[n % 2, :, :LANES] / acc_ref[n % 2, :, LANES:LANES + 1])

    lam = (jnp.exp(jnp.sum(lq1_ref[...] * lk1_ref[...], axis=-1, keepdims=True))
           - jnp.exp(jnp.sum(lq2_ref[...] * lk2_ref[...], axis=-1, keepdims=True))
           + LAM_INIT)
    for blk in range(DF_BLOCKS):
        o = outs[2 * blk] - lam * outs[2 * blk + 1]
        y = o * lax.rsqrt(jnp.mean(o * o, axis=-1, keepdims=True) + LN_EPS)
        o_ref[blk * tq:(blk + 1) * tq, :] = (y * g_ref[...] * (1.0 - LAM_INIT)).astype(
            o_ref.dtype)


def _na_table_kernel(rpb_ref, o_ref):
    head = pl.program_id(0)
    n_dr, n_dc = 2 * NA_ROWS - 1, 2 * NA_COLS - 1
    c = lax.broadcasted_iota(jnp.int32, (GRID_W, LANES), 0)
    lane = lax.broadcasted_iota(jnp.int32, (GRID_W, LANES), 1)
    cc = lane & (GRID_W - 1)
    c0 = jnp.clip(c - NA_COLS // 2, 0, GRID_W - NA_COLS)
    in_window = (cc >= c0) & (cc < c0 + NA_COLS)
    dc = jnp.where(in_window, cc - c + (NA_COLS - 1), -1)
    upper = lane >= GRID_W
    for d in range(n_dr - 1):
        base = (head * n_dr + d) * n_dc
        acc = jnp.full((GRID_W, LANES), NEG_BIG, F32)
        for x in range(n_dc):
            val = jnp.where(upper, rpb_ref[base + n_dc + x], rpb_ref[base + x])
            acc = jnp.where(dc == x, val, acc)
        o_ref[0, d] = acc * LOG2E


def _na_bias_table(rpb):
    shape = (NA_HEADS, 2 * NA_ROWS - 2, GRID_W, LANES)
    return pl.pallas_call(
        _na_table_kernel,
        grid=(NA_HEADS,),
        in_specs=[pl.BlockSpec(memory_space=pltpu.SMEM)],
        out_specs=pl.BlockSpec((1, *shape[1:]), lambda h: (h, 0, 0, 0)),
        out_shape=jax.ShapeDtypeStruct(shape, F32),
        compiler_params=_params(1),
        name="na_bias_table",
    )(rpb.reshape(-1))


def _resident(shape):
    return pl.BlockSpec(shape, lambda *_: (0,) * len(shape), pipeline_mode=pl.Buffered(1))


def _params(n_axes):
    return pltpu.CompilerParams(dimension_semantics=("arbitrary",) * n_axes,
                                vmem_limit_bytes=VMEM_LIMIT)


def _ffn_weight_specs():
    return [_resident((D_MODEL, D_FF)), _resident((D_MODEL, D_FF)), _resident((D_FF, D_MODEL))]


def _ffn1_call(x2d, wg, wu, wd, g, b, w_in):
    n = x2d.shape[0]
    row_spec = pl.BlockSpec((FFN_ROWS, D_MODEL), lambda i: (i, 0))
    vec_spec = _resident((1, D_MODEL))
    return pl.pallas_call(
        _ffn1_kernel,
        grid=(n // FFN_ROWS,),
        in_specs=[row_spec, *_ffn_weight_specs(), vec_spec, vec_spec,
                  _resident((D_MODEL, IN_COLS))],
        out_specs=[row_spec, pl.BlockSpec((FFN_ROWS, IN_COLS), lambda i: (i, 0))],
        out_shape=[jax.ShapeDtypeStruct((n, D_MODEL), F32),
                   jax.ShapeDtypeStruct((n, IN_COLS), BF16)],
        scratch_shapes=[pltpu.VMEM((FFN_ROWS, D_MODEL), BF16),
                        pltpu.VMEM((FFN_ROWS, D_MODEL), F32)],
        compiler_params=_params(1),
        name="ffn1_ln1_qkv",
    )(x2d, wg, wu, wd, g, b, w_in)


def _ffn2_call(x1, na_out, df_out, w_out, g2, b2, wg, wu, wd, g3, b3):
    n = x1.shape[0]
    row_spec = pl.BlockSpec((FFN_ROWS, D_MODEL), lambda i: (i, 0))
    half_spec = pl.BlockSpec((FFN_ROWS, NA_WIDTH), lambda i: (i, 0))
    vec_spec = _resident((1, D_MODEL))
    return pl.pallas_call(
        _ffn2_kernel,
        grid=(n // FFN_ROWS,),
        in_specs=[row_spec, half_spec, half_spec, _resident((D_MODEL, D_MODEL)),
                  vec_spec, vec_spec, *_ffn_weight_specs(), vec_spec, vec_spec],
        out_specs=row_spec,
        out_shape=jax.ShapeDtypeStruct((n, D_MODEL), F32),
        scratch_shapes=[pltpu.VMEM((FFN_ROWS, D_MODEL), F32),
                        pltpu.VMEM((FFN_ROWS, D_MODEL), BF16),
                        pltpu.VMEM((FFN_ROWS, D_MODEL), F32)],
        compiler_params=_params(1),
        name="outproj_ln2_ffn2_ln3",
    )(x1, na_out, df_out, w_out, g2, b2, wg, wu, wd, g3, b3)


def _na_call(p, tbl, batch, seq):
    step_rows = NA_ROWS_PER_STEP * GRID_W
    steps = seq // step_rows
    q_spec = pl.BlockSpec((step_rows, NA_WIDTH), lambda b, i: (b * steps + i, 0))
    return pl.pallas_call(
        _na_kernel,
        grid=(batch, steps),
        in_specs=[q_spec,
                  pl.BlockSpec((seq, NA_WIDTH), lambda b, i: (b, 1)),
                  pl.BlockSpec((seq, NA_WIDTH), lambda b, i: (b, 2)),
                  _resident(tbl.shape)],
        out_specs=q_spec,
        out_shape=jax.ShapeDtypeStruct((batch * seq, NA_WIDTH), BF16),
        compiler_params=_params(2),
        name="neighbourhood_attention",
    )(p, p, p, tbl)


def _diff_call(p, slopes, lq1, lk1, lq2, lk2, subln_g, batch, seq):
    step_rows = DF_BLOCKS * DF_TQ
    steps = seq // step_rows
    q_col0 = 3 * NA_WIDTH // LANES
    k_col0 = q_col0 + DF_WIDTH // LANES
    v_col0 = k_col0 + DF_WIDTH // LANES
    lam_spec = _resident((1, DF_DH))
    return pl.pallas_call(
        _diff_kernel,
        grid=(DF_HEADS, batch, steps),
        in_specs=[pl.BlockSpec(memory_space=pltpu.SMEM),
                  pl.BlockSpec((step_rows, LANES), lambda h, b, i: (b * steps + i, q_col0 + h)),
                  pl.BlockSpec((seq, LANES), lambda h, b, i: (b, k_col0 + h)),
                  pl.BlockSpec((seq, LANES), lambda h, b, i: (b, v_col0 + h)),
                  lam_spec, lam_spec, lam_spec, lam_spec,
                  _resident((1, 2 * DF_DH))],
        out_specs=pl.BlockSpec((step_rows, LANES), lambda h, b, i: (b * steps + i, h)),
        out_shape=jax.ShapeDtypeStruct((batch * seq, DF_WIDTH), BF16),
        scratch_shapes=[pltpu.VMEM((seq, 2 * LANES), BF16),
                        pltpu.VMEM((seq, 2 * LANES), BF16),
                        pltpu.VMEM((DF_BLOCKS, 2, 3, DF_TQ, 2 * LANES), BF16),
                        pltpu.VMEM((DF_TK // DF_TQ, DF_TQ, DF_TK), F32),
                        pltpu.VMEM((2, DF_TQ, seq), F32),
                        pltpu.VMEM((2, DF_TQ, LANES), F32),
                        pltpu.VMEM((2, DF_TQ, 2 * LANES), F32)],
        compiler_params=_params(3),
        name="differential_attention",
    )(slopes, p, p, p, lq1, lk1, lq2, lk2, subln_g)


def kernel(x, ln1_g, ln1_b, ffn1_w_gate, ffn1_w_up, ffn1_w_down, w_in, na_rpb,
           diff_lambda_q1, diff_lambda_k1, diff_lambda_q2, diff_lambda_k2, diff_subln_g,
           w_out, ln2_g, ln2_b, ffn2_w_gate, ffn2_w_up, ffn2_w_down, ln3_g, ln3_b):
    batch, seq, d = x.shape
    assert d == D_MODEL and seq % (NA_ROWS_PER_STEP * GRID_W) == 0
    assert seq % DF_TK == 0 and DF_TK % DF_TQ == 0 and seq % (DF_BLOCKS * DF_TQ) == 0
    assert ln1_g.shape[0] == DEPTH == 1
    x2d = x.reshape(batch * seq, d)
    x1, p = _ffn1_call(x2d, ffn1_w_gate[0].astype(BF16), ffn1_w_up[0].astype(BF16),
                       ffn1_w_down[0].astype(BF16), ln1_g, ln1_b, w_in[0].astype(BF16))
    na_out = _na_call(p, _na_bias_table(na_rpb[0]), batch, seq)
    slopes = jnp.exp2(-8.0 * jnp.arange(1, DF_HEADS + 1, dtype=F32) / DF_HEADS)
    df_out = _diff_call(p, slopes, diff_lambda_q1, diff_lambda_k1, diff_lambda_q2,
                        diff_lambda_k2, diff_subln_g, batch, seq)
    out = _ffn2_call(x1, na_out, df_out, w_out[0].astype(BF16), ln2_g, ln2_b,
                     ffn2_w_gate[0].astype(BF16), ffn2_w_up[0].astype(BF16),
                     ffn2_w_down[0].astype(BF16), ln3_g, ln3_b)
    return out.reshape(batch, seq, d)
```

```python
import math

import jax
import jax.numpy as jnp
from jax import lax
from jax.experimental import pallas as pl
from jax.experimental.pallas import tpu as pltpu

F32 = jnp.float32
BF16 = jnp.bfloat16

D_MODEL = 1024
D_FF = 2816
DEPTH = 1
GRID_W = 64
NA_HEADS, NA_DH = 8, 64
NA_ROWS, NA_COLS = 8, 16
DF_HEADS, DF_DH = 4, 64
NA_WIDTH = NA_HEADS * NA_DH
DF_WIDTH = DF_HEADS * 2 * DF_DH
IN_COLS = 3 * NA_WIDTH + 3 * DF_WIDTH
ALPHA = (2.0 * DEPTH) ** 0.25
LN_EPS = 1e-5
LAM_INIT = 0.8 - 0.6 * math.exp(-0.3 * 0)
NEG_BIG = -1e30
LOG2E = math.log2(math.e)

LANES = 128
MXU_DIM = 256
FF_CHUNK = MXU_DIM
N_FF_CHUNKS = D_FF // FF_CHUNK
FFN_ROWS = 512
FFN_SUB = 256
FFN_SKEW = 2
NA_ROWS_PER_STEP = 16
DF_TQ = 256
DF_TK = 256
DF_BLOCKS = 8
ALIBI_TERMS = 3
VMEM_LIMIT = 56 * 1024 * 1024


def _layer_norm(y, g, b):
    mu = jnp.mean(y, axis=-1, keepdims=True)
    yc = y - mu
    var = jnp.mean(yc * yc, axis=-1, keepdims=True)
    return yc * lax.rsqrt(var + LN_EPS) * g + b


def _swiglu_lanes(xb_ref, wg_ref, wu_ref, wd_ref, acc_ref, before_lane=None, after_lane=None):
    lanes = [slice(r, r + FFN_SUB) for r in range(0, FFN_ROWS, FFN_SUB)]
    pending = [None] * len(lanes)
    if before_lane is not None:
        for rows in lanes:
            before_lane(rows)

    def slot(k, s):
        rows = lanes[k]
        h_new = None
        if s < N_FF_CHUNKS:
            xb = xb_ref[rows, :]
            cols = slice(s * FF_CHUNK, (s + 1) * FF_CHUNK)
            g = jnp.dot(xb, wg_ref[:, cols], preferred_element_type=F32)
            u = jnp.dot(xb, wu_ref[:, cols], preferred_element_type=F32)
            h_new = (g * jax.nn.sigmoid(g) * u).astype(BF16)
        if s >= 1:
            d = jnp.dot(pending[k], wd_ref[(s - 1) * FF_CHUNK:s * FF_CHUNK, :],
                        preferred_element_type=F32)
            if s == 1:
                acc_ref[rows, :] = d
            else:
                acc_ref[rows, :] += d
        pending[k] = h_new

    n_slots = N_FF_CHUNKS + 1
    finished = []
    for t in range(n_slots + FFN_SKEW * (len(lanes) - 1)):
        for k in range(len(lanes)):
            s = t - FFN_SKEW * k
            if 0 <= s < n_slots:
                slot(k, s)
                if s == n_slots - 1:
                    finished.append(k)
    if after_lane is not None:
        for k in finished:
            after_lane(lanes[k])


def _ffn1_kernel(x_ref, wg_ref, wu_ref, wd_ref, g_ref, b_ref, win_ref,
                 x1_ref, p_ref, xb_ref, acc_ref):
    xb_ref[...] = x_ref[...].astype(BF16)

    def project(rows):
        x1 = _layer_norm(ALPHA * x_ref[rows, :] + 0.5 * acc_ref[rows, :],
                         g_ref[...], b_ref[...])
        x1_ref[rows, :] = x1
        x1b = x1.astype(BF16)
        group_scale = (NA_DH ** -0.5 * LOG2E, None, None, DF_DH ** -0.5 * LOG2E, None, None)
        for j, gs in enumerate(group_scale):
            cols = slice(j * NA_WIDTH, (j + 1) * NA_WIDTH)
            pj = jnp.dot(x1b, win_ref[:, cols], preferred_element_type=F32)
            if gs is not None:
                pj = pj * gs
            p_ref[rows, cols] = pj.astype(BF16)

    _swiglu_lanes(xb_ref, wg_ref, wu_ref, wd_ref, acc_ref, after_lane=project)


def _ffn2_kernel(x1_ref, na_ref, df_ref, wout_ref, g2_ref, b2_ref,
                 wg_ref, wu_ref, wd_ref, g3_ref, b3_ref,
                 o_ref, x2_ref, xb_ref, acc_ref):
    def mix_in(rows):
        mix = jnp.dot(na_ref[rows, :], wout_ref[:NA_WIDTH, :], preferred_element_type=F32)
        mix += jnp.dot(df_ref[rows, :], wout_ref[NA_WIDTH:, :], preferred_element_type=F32)
        x2 = _layer_norm(ALPHA * x1_ref[rows, :] + mix, g2_ref[...], b2_ref[...])
        x2_ref[rows, :] = x2
        xb_ref[rows, :] = x2.astype(BF16)

    def finish(rows):
        o_ref[rows, :] = _layer_norm(ALPHA * x2_ref[rows, :] + 0.5 * acc_ref[rows, :],
                                     g3_ref[...], b3_ref[...])

    _swiglu_lanes(xb_ref, wg_ref, wu_ref, wd_ref, acc_ref, before_lane=mix_in,
                  after_lane=finish)


def _na_kernel(q_ref, k_ref, v_ref, tbl_ref, o_ref):
    rows_total = k_ref.shape[0] // GRID_W
    win_keys = NA_ROWS * GRID_W
    lane = lax.broadcasted_iota(jnp.int32, (GRID_W, LANES), 1)
    lo = lane < NA_DH

    def window(rr):
        r = pl.program_id(1) * NA_ROWS_PER_STEP + rr
        r0 = jnp.clip(r - NA_ROWS // 2, 0, rows_total - NA_ROWS)
        return pl.ds(pl.multiple_of(r0 * GRID_W, GRID_W), win_keys), r - r0

    def scores(rr):
        krows, e = window(rr)
        out = []
        for hp in range(NA_HEADS // 2):
            cols = slice(hp * LANES, (hp + 1) * LANES)
            q = q_ref[rr * GRID_W:(rr + 1) * GRID_W, cols]
            zero = jnp.zeros_like(q)
            qz = jnp.concatenate([jnp.where(lo, q, zero), jnp.where(lo, zero, q)], axis=0)
            s = lax.dot_general(qz, k_ref[krows, cols], (((1,), (1,)), ((), ())),
                                preferred_element_type=F32)
            bias = jnp.concatenate([jnp.concatenate(
                [tbl_ref[2 * hp + par, 2 * w - e + NA_ROWS - 1] for w in range(NA_ROWS // 2)],
                axis=1) for par in range(2)], axis=0)
            out.append(s + bias)
        return out

    def finish(rr, s_list):
        krows, _ = window(rr)
        for hp, s in enumerate(s_list):
            cols = slice(hp * LANES, (hp + 1) * LANES)
            m = jnp.max(s, axis=-1, keepdims=True)
            p = jnp.exp2(s - m)
            l = jnp.sum(p, axis=-1, keepdims=True)
            o = jnp.dot(p.astype(BF16), v_ref[krows, cols], preferred_element_type=F32)
            o = o / l
            o_ref[rr * GRID_W:(rr + 1) * GRID_W, cols] = jnp.where(
                lo, o[:GRID_W], o[GRID_W:]).astype(o_ref.dtype)

    pending = scores(0)
    for rr in range(NA_ROWS_PER_STEP):
        upcoming = scores(rr + 1) if rr + 1 < NA_ROWS_PER_STEP else None
        finish(rr, pending)
        pending = upcoming


def _bf16_terms(x):
    terms, rest = [], x
    for _ in range(ALIBI_TERMS):
        terms.append(rest.astype(BF16).astype(F32))
        rest = rest - terms[-1]
    return terms


def _diff_kernel(slope_ref, q_ref, k_ref, v_ref, lq1_ref, lk1_ref, lq2_ref, lk2_ref, g_ref,
                 o_ref, kp_ref, vp_ref, qv_ref, diag_ref, t_ref, mx_ref, acc_ref):
    tq, tk = DF_TQ, DF_TK
    n_chunks = k_ref.shape[0] // tk
    ratio = tk // tq
    groups = tk // LANES
    h = pl.program_id(0)
    b = pl.program_id(1)
    i = pl.program_id(2)
    a = slope_ref[h] * LOG2E
    lane = lax.broadcasted_iota(jnp.int32, (tq, LANES), 1)

    @pl.when((i == 0) & (b == 0))
    def _per_head_setup():
        t_keys = k_ref.shape[0]
        klane = lax.broadcasted_iota(jnp.int32, (t_keys, LANES), 1)
        kpos = lax.broadcasted_iota(jnp.int32, (t_keys, LANES), 0).astype(F32)
        feat = jnp.where(klane < ALIBI_TERMS, 1.0, 0.0)
        for t, term in enumerate(_bf16_terms(a * kpos)):
            feat = jnp.where(klane == ALIBI_TERMS + t, term, feat)
        kp_ref[:, LANES:] = feat.astype(BF16)
        vp_ref[:, LANES:] = jnp.where(klane == 0, 1.0, 0.0).astype(BF16)
        rel = (lax.broadcasted_iota(jnp.int32, (tq, tk), 0)
               - lax.broadcasted_iota(jnp.int32, (tq, tk), 1)).astype(F32)
        for par in range(ratio):
            diag_ref[par] = -a * jnp.abs(rel + float(par * tq))

    @pl.when(i == 0)
    def _per_sequence_setup():
        kp_ref[:, :LANES] = k_ref[...]
        vp_ref[:, :LANES] = v_ref[...]

    block_info = []
    for blk in range(DF_BLOCKS):
        ib = i * DF_BLOCKS + blk
        q = q_ref[blk * tq:(blk + 1) * tq, :]
        zero = jnp.zeros_like(q)
        qpos = (ib * tq + lax.broadcasted_iota(jnp.int32, (tq, LANES), 0)).astype(F32)
        before = jnp.where(lane < 2 * ALIBI_TERMS, 1.0, 0.0)
        for t, term in enumerate(_bf16_terms(-a * qpos)):
            before = jnp.where(lane == t, term, before)
        before = before.astype(BF16)
        for half in range(2):
            qz = jnp.where(lane < DF_DH, q, zero) if half == 0 else jnp.where(
                lane < DF_DH, zero, q)
            for kind, feat in enumerate((before, -before, zero)):
                qv_ref[blk, half, kind, :, :LANES] = qz
                qv_ref[blk, half, kind, :, LANES:] = feat
        block_info.append((lax.div(ib, ratio), lax.rem(ib, ratio)))

    def chunk_rows(blk, d):
        j_diag = block_info[blk][0]
        j = lax.rem(j_diag + d, n_chunks)
        return j, pl.ds(pl.multiple_of(j * tk, tk), tk)

    def lane_groups(x):
        return [x[:, g * LANES:(g + 1) * LANES] for g in range(groups)]

    def score_chunk(n, d):
        blk, half = divmod(n, 2)
        j_diag, par = block_info[blk]
        j, rows = chunk_rows(blk, d)
        dims = (((1,), (1,)), ((), ()))
        if d == 0:
            t = lax.dot_general(qv_ref[blk, half, 2], kp_ref[rows, :], dims,
                                preferred_element_type=F32) + diag_ref[par]
        else:
            kind = jnp.where(j < j_diag, 0, 1)
            t = lax.dot_general(qv_ref[blk, half, kind], kp_ref[rows, :], dims,
                                preferred_element_type=F32)
        t_ref[n % 2, :, d * tk:(d + 1) * tk] = t
        tg = lane_groups(t)
        while len(tg) > 1:
            tg = [jnp.maximum(x, y) for x, y in zip(tg[::2], tg[1::2])]
        mx_ref[n % 2] = tg[0] if d == 0 else jnp.maximum(mx_ref[n % 2], tg[0])

    def value_chunk(n, d, m):
        _, rows = chunk_rows(n // 2, d)
        pg = [jnp.exp2(x - m) for x in lane_groups(t_ref[n % 2, :, d * tk:(d + 1) * tk])]
        p = jnp.concatenate([x.astype(BF16) for x in pg], axis=1)
        pv = jnp.dot(p, vp_ref[rows, :], preferred_element_type=F32)
        if d == 0:
            acc_ref[n % 2] = pv
        else:
            acc_ref[n % 2] += pv

    n_sets = 2 * DF_BLOCKS
    for d in range(n_chunks):
        score_chunk(0, d)
    outs = []
    for n in range(n_sets):
        m = jnp.broadcast_to(jnp.max(mx_ref[n % 2], axis=-1, keepdims=True), (tq, LANES))
        for d in range(n_chunks):
            if n + 1 < n_sets:
                score_chunk(n + 1, d)
            value_chunk(n, d, m)
        outs.append(acc_ref[n % 2, :, :LANES] / acc_ref[n % 2, :, LANES:LANES + 1])

    lam = (jnp.exp(jnp.sum(lq1_ref[...] * lk1_ref[...], axis=-1, keepdims=True))
           - jnp.exp(jnp.sum(lq2_ref[...] * lk2_ref[...], axis=-1, keepdims=True))
           + LAM_INIT)
    for blk in range(DF_BLOCKS):
        o = outs[2 * blk] - lam * outs[2 * blk + 1]
        y = o * lax.rsqrt(jnp.mean(o * o, axis=-1, keepdims=True) + LN_EPS)
        o_ref[blk * tq:(blk + 1) * tq, :] = (y * g_ref[...] * (1.0 - LAM_INIT)).astype(
            o_ref.dtype)


def _na_table_kernel(rpb_ref, o_ref):
    head = pl.program_id(0)
    n_dr, n_dc = 2 * NA_ROWS - 1, 2 * NA_COLS - 1
    c = lax.broadcasted_iota(jnp.int32, (GRID_W, LANES), 0)
    lane = lax.broadcasted_iota(jnp.int32, (GRID_W, LANES), 1)
    cc = lane & (GRID_W - 1)
    c0 = jnp.clip(c - NA_COLS // 2, 0, GRID_W - NA_COLS)
    in_window = (cc >= c0) & (cc < c0 + NA_COLS)
    dc = jnp.where(in_window, cc - c + (NA_COLS - 1), -1)
    upper = lane >= GRID_W
    for d in range(n_dr - 1):
        base = (head * n_dr + d) * n_dc
        acc = jnp.full((GRID_W, LANES), NEG_BIG, F32)
        for x in range(n_dc):
            val = jnp.where(upper, rpb_ref[base + n_dc + x], rpb_ref[base + x])
            acc = jnp.where(dc == x, val, acc)
        o_ref[0, d] = acc * LOG2E


def _na_bias_table(rpb):
    shape = (NA_HEADS, 2 * NA_ROWS - 2, GRID_W, LANES)
    return pl.pallas_call(
        _na_table_kernel,
        grid=(NA_HEADS,),
        in_specs=[pl.BlockSpec(memory_space=pltpu.SMEM)],
        out_specs=pl.BlockSpec((1, *shape[1:]), lambda h: (h, 0, 0, 0)),
        out_shape=jax.ShapeDtypeStruct(shape, F32),
        compiler_params=_params(1),
        name="na_bias_table",
    )(rpb.reshape(-1))


def _resident(shape):
    return pl.BlockSpec(shape, lambda *_: (0,) * len(shape), pipeline_mode=pl.Buffered(1))


def _params(n_axes):
    return pltpu.CompilerParams(dimension_semantics=("arbitrary",) * n_axes,
                                vmem_limit_bytes=VMEM_LIMIT)


def _ffn_weight_specs():
    return [_resident((D_MODEL, D_FF)), _resident((D_MODEL, D_FF)), _resident((D_FF, D_MODEL))]


def _ffn1_call(x2d, wg, wu, wd, g, b, w_in):
    n = x2d.shape[0]
    row_spec = pl.BlockSpec((FFN_ROWS, D_MODEL), lambda i: (i, 0))
    vec_spec = _resident((1, D_MODEL))
    return pl.pallas_call(
        _ffn1_kernel,
        grid=(n // FFN_ROWS,),
        in_specs=[row_spec, *_ffn_weight_specs(), vec_spec, vec_spec,
                  _resident((D_MODEL, IN_COLS))],
        out_specs=[row_spec, pl.BlockSpec((FFN_ROWS, IN_COLS), lambda i: (i, 0))],
        out_shape=[jax.ShapeDtypeStruct((n, D_MODEL), F32),
                   jax.ShapeDtypeStruct((n, IN_COLS), BF16)],
        scratch_shapes=[pltpu.VMEM((FFN_ROWS, D_MODEL), BF16),
                        pltpu.VMEM((FFN_ROWS, D_MODEL), F32)],
        compiler_params=_params(1),
        name="ffn1_ln1_qkv",
    )(x2d, wg, wu, wd, g, b, w_in)


def _ffn2_call(x1, na_out, df_out, w_out, g2, b2, wg, wu, wd, g3, b3):
    n = x1.shape[0]
    row_spec = pl.BlockSpec((FFN_ROWS, D_MODEL), lambda i: (i, 0))
    half_spec = pl.BlockSpec((FFN_ROWS, NA_WIDTH), lambda i: (i, 0))
    vec_spec = _resident((1, D_MODEL))
    return pl.pallas_call(
        _ffn2_kernel,
        grid=(n // FFN_ROWS,),
        in_specs=[row_spec, half_spec, half_spec, _resident((D_MODEL, D_MODEL)),
                  vec_spec, vec_spec, *_ffn_weight_specs(), vec_spec, vec_spec],
        out_specs=row_spec,
        out_shape=jax.ShapeDtypeStruct((n, D_MODEL), F32),
        scratch_shapes=[pltpu.VMEM((FFN_ROWS, D_MODEL), F32),
                        pltpu.VMEM((FFN_ROWS, D_MODEL), BF16),
                        pltpu.VMEM((FFN_ROWS, D_MODEL), F32)],
        compiler_params=_params(1),
        name="outproj_ln2_ffn2_ln3",
    )(x1, na_out, df_out, w_out, g2, b2, wg, wu, wd, g3, b3)


def _na_call(p, tbl, batch, seq):
    step_rows = NA_ROWS_PER_STEP * GRID_W
    steps = seq // step_rows
    q_spec = pl.BlockSpec((step_rows, NA_WIDTH), lambda b, i: (b * steps + i, 0))
    return pl.pallas_call(
        _na_kernel,
        grid=(batch, steps),
        in_specs=[q_spec,
                  pl.BlockSpec((seq, NA_WIDTH), lambda b, i: (b, 1)),
                  pl.BlockSpec((seq, NA_WIDTH), lambda b, i: (b, 2)),
                  _resident(tbl.shape)],
        out_specs=q_spec,
        out_shape=jax.ShapeDtypeStruct((batch * seq, NA_WIDTH), BF16),
        compiler_params=_params(2),
        name="neighbourhood_attention",
    )(p, p, p, tbl)


def _diff_call(p, slopes, lq1, lk1, lq2, lk2, subln_g, batch, seq):
    step_rows = DF_BLOCKS * DF_TQ
    steps = seq // step_rows
    q_col0 = 3 * NA_WIDTH // LANES
    k_col0 = q_col0 + DF_WIDTH // LANES
    v_col0 = k_col0 + DF_WIDTH // LANES
    lam_spec = _resident((1, DF_DH))
    return pl.pallas_call(
        _diff_kernel,
        grid=(DF_HEADS, batch, steps),
        in_specs=[pl.BlockSpec(memory_space=pltpu.SMEM),
                  pl.BlockSpec((step_rows, LANES), lambda h, b, i: (b * steps + i, q_col0 + h)),
                  pl.BlockSpec((seq, LANES), lambda h, b, i: (b, k_col0 + h)),
                  pl.BlockSpec((seq, LANES), lambda h, b, i: (b, v_col0 + h)),
                  lam_spec, lam_spec, lam_spec, lam_spec,
                  _resident((1, 2 * DF_DH))],
        out_specs=pl.BlockSpec((step_rows, LANES), lambda h, b, i: (b * steps + i, h)),
        out_shape=jax.ShapeDtypeStruct((batch * seq, DF_WIDTH), BF16),
        scratch_shapes=[pltpu.VMEM((seq, 2 * LANES), BF16),
                        pltpu.VMEM((seq, 2 * LANES), BF16),
                        pltpu.VMEM((DF_BLOCKS, 2, 3, DF_TQ, 2 * LANES), BF16),
                        pltpu.VMEM((DF_TK // DF_TQ, DF_TQ, DF_TK), F32),
                        pltpu.VMEM((2, DF_TQ, seq), F32),
                        pltpu.VMEM((2, DF_TQ, LANES), F32),
                        pltpu.VMEM((2, DF_TQ, 2 * LANES), F32)],
        compiler_params=_params(3),
        name="differential_attention",
    )(slopes, p, p, p, lq1, lk1, lq2, lk2, subln_g)


def kernel(x, ln1_g, ln1_b, ffn1_w_gate, ffn1_w_up, ffn1_w_down, w_in, na_rpb,
           diff_lambda_q1, diff_lambda_k1, diff_lambda_q2, diff_lambda_k2, diff_subln_g,
           w_out, ln2_g, ln2_b, ffn2_w_gate, ffn2_w_up, ffn2_w_down, ln3_g, ln3_b):
    batch, seq, d = x.shape
    assert d == D_MODEL and seq % (NA_ROWS_PER_STEP * GRID_W) == 0
    assert seq % DF_TK == 0 and DF_TK % DF_TQ == 0 and seq % (DF_BLOCKS * DF_TQ) == 0
    assert ln1_g.shape[0] == DEPTH == 1
    x2d = x.reshape(batch * seq, d)
    x1, p = _ffn1_call(x2d, ffn1_w_gate[0].astype(BF16), ffn1_w_up[0].astype(BF16),
                       ffn1_w_down[0].astype(BF16), ln1_g, ln1_b, w_in[0].astype(BF16))
    na_out = _na_call(p, _na_bias_table(na_rpb[0]), batch, seq)
    slopes = jnp.exp2(-8.0 * jnp.arange(1, DF_HEADS + 1, dtype=F32) / DF_HEADS)
    df_out = _diff_call(p, slopes, diff_lambda_q1, diff_lambda_k1, diff_lambda_q2,
                        diff_lambda_k2, diff_subln_g, batch, seq)
    out = _ffn2_call(x1, na_out, df_out, w_out[0].astype(BF16), ln2_g, ln2_b,
                     ffn2_w_gate[0].astype(BF16), ffn2_w_up[0].astype(BF16),
                     ffn2_w_down[0].astype(BF16), ln3_g, ln3_b)
    return out.reshape(batch, seq, d)
```
